```python
import jax
import jax.numpy as jnp
from jax import lax
import numpy as np

D_MODEL = 1024
BATCH = 2
SEQ = 8192
DEPTH = 2
DEC_BATCH = 128
DEC_SEQ = 4
PAST_LEN = 8192
PAGE_SIZE = 128

N_EVEN = (DEPTH + 1) // 2
N_ODD = DEPTH // 2
D_FF = 2816
PLE_DIM = 256
H_A = 4
DK_A = 128
DV_A = 128
RET_CHUNK = 128
H_B = 8
NOPE_B = 64
ROPE_B = 32
VD_B = 64
Q_RANK = 256
KV_RANK = 128
H_C = 16
HD_C = 64
Q_BLOCK = 128
ROPE_THETA = 10000.0
NORM_EPS = 1e-6
NEG_INF = -1e30
FORGET_BIAS = 2.0
MLA_SCALE = (NOPE_B + ROPE_B) ** -0.5
FOX_SCALE = HD_C ** -0.5
IN0_WIDTHS = (H_A * DK_A, H_A * DK_A, H_A * DV_A, H_A * DV_A, Q_RANK, KV_RANK, ROPE_B)
IN0_DIM = 2 * H_A * DK_A + 2 * H_A * DV_A + Q_RANK + KV_RANK + ROPE_B
OUT0_DIM = H_A * DV_A + H_B * VD_B
IN1_WIDTHS = (H_C * HD_C, H_C * HD_C, H_C * HD_C, H_C)
IN1_DIM = 3 * H_C * HD_C + H_C

kernel_name = "hybrid_retention_mla_fox_decoder_step"


def _split(z, widths):
    idx = np.cumsum(widths)[:-1].tolist()
    return jnp.split(z, idx, axis=-1)


def _rmsnorm(x, g):
    xf = x.astype(jnp.float32)
    y = xf * lax.rsqrt(jnp.mean(xf * xf, axis=-1, keepdims=True) + NORM_EPS)
    return (y * g.astype(jnp.float32)).astype(x.dtype)


def _head_norm(x, g):
    xf = x.astype(jnp.float32)
    mu = jnp.mean(xf, axis=-1, keepdims=True)
    xc = xf - mu
    var = jnp.mean(xc * xc, axis=-1, keepdims=True)
    return (xc * lax.rsqrt(var + NORM_EPS) * g.astype(jnp.float32)).astype(x.dtype)


def _rope(x, pos):
    half = x.shape[-1] // 2
    inv = ROPE_THETA ** (-jnp.arange(half, dtype=jnp.float32) / half)
    ang = pos.astype(jnp.float32)[:, None] * inv[None, :]
    cos = jnp.cos(ang)[None, :, None, :].astype(x.dtype)
    sin = jnp.sin(ang)[None, :, None, :].astype(x.dtype)
    x1, x2 = x[..., :half], x[..., half:]
    return jnp.concatenate([x1 * cos - x2 * sin, x1 * sin + x2 * cos], axis=-1)


def _swiglu(h, w_in, w_out):
    a, b = jnp.split(h @ w_in, 2, axis=-1)
    return (jax.nn.silu(a) * b) @ w_out


def _retention_chunk(q, k, v, s0, log_gamma):
    L = q.shape[1]
    idx = jnp.arange(L, dtype=jnp.float32)
    diff = idx[:, None] - idx[None, :]
    decay = jnp.where(diff >= 0, jnp.exp(log_gamma[:, None, None] * jnp.maximum(diff, 0.0)), 0.0).astype(q.dtype)
    q_decay = jnp.exp(log_gamma[None, :] * (idx[:, None] + 1.0)).astype(q.dtype)
    k_decay = jnp.exp(log_gamma[None, :] * (L - 1.0 - idx[:, None])).astype(q.dtype)
    s_decay = jnp.exp(log_gamma * L).astype(q.dtype)
    scores = jnp.einsum('bihd,bjhd->bhij', q, k) * decay
    o = (jnp.einsum('bhij,bjhe->bihe', scores, v)
         + jnp.einsum('bihd,bhde->bihe', q, s0) * q_decay[None, :, :, None])
    s_new = s_decay[None, :, None, None] * s0 + jnp.einsum('bjhd,bjhe->bhde', k * k_decay[None, :, :, None], v)
    return o, s_new


def _retention_prompt(q, k, v, log_gamma):
    B, S, H, _ = q.shape
    nc = S // RET_CHUNK

    def to_chunks(t):
        return t.reshape((B, nc, RET_CHUNK) + t.shape[2:]).swapaxes(0, 1)

    s0 = jnp.zeros((B, H, DK_A, DV_A), q.dtype)

    def step(s, qkv):
        o, s = _retention_chunk(qkv[0], qkv[1], qkv[2], s, log_gamma)
        return s, o

    s_fin, o = lax.scan(step, s0, (to_chunks(q), to_chunks(k), to_chunks(v)))
    return o.swapaxes(0, 1).reshape(B, S, H, DV_A), s_fin


def _mla_prompt(q_lat, q_pe, ckv, kpe):
    B, S, H, R = q_lat.shape
    nb = S // Q_BLOCK
    qb = q_lat.reshape(B, nb, Q_BLOCK, H, R).swapaxes(0, 1)
    pb = q_pe.reshape(B, nb, Q_BLOCK, H, ROPE_B).swapaxes(0, 1)
    kpos = jnp.arange(S)

    def blk(args):
        ql, qp, i = args
        s = (jnp.einsum('bqhr,bkr->bhqk', ql, ckv)
             + jnp.einsum('bqhn,bkn->bhqk', qp, kpe)).astype(jnp.float32) * MLA_SCALE
        qpos = i * Q_BLOCK + jnp.arange(Q_BLOCK)
        s = jnp.where(kpos[None, :] <= qpos[:, None], s, NEG_INF)
        p = jax.nn.softmax(s, axis=-1)
        return jnp.einsum('bhqk,bkr->bqhr', p.astype(ckv.dtype), ckv)

    o = lax.map(blk, (qb, pb, jnp.arange(nb)))
    return o.swapaxes(0, 1).reshape(B, S, H, R)


def _mla_sample(q_lat, q_pe, ckv_new, kpe_new, pool_ckv, pool_kpe, j, page_table):
    T = q_lat.shape[1]

    def one(args):
        ql, qp, cn, kn, pages = args
        ck = jnp.concatenate([pool_ckv[j, pages].reshape(-1, KV_RANK).astype(cn.dtype), cn], axis=0)
        kp = jnp.concatenate([pool_kpe[j, pages].reshape(-1, ROPE_B).astype(kn.dtype), kn], axis=0)
        past = ck.shape[0] - T
        s = (jnp.einsum('thr,kr->htk', ql, ck)
             + jnp.einsum('thn,kn->htk', qp, kp)).astype(jnp.float32) * MLA_SCALE
        mask = jnp.arange(past + T)[None, :] <= past + jnp.arange(T)[:, None]
        p = jax.nn.softmax(jnp.where(mask, s, NEG_INF), axis=-1)
        return jnp.einsum('htk,kr->thr', p.astype(ck.dtype), ck)

    return lax.map(one, (q_lat, q_pe, ckv_new, kpe_new, page_table))


def _fox_prompt(q, k, v, log_f):
    B, S, H, D = q.shape
    c = jnp.cumsum(log_f, axis=1)
    nb = S // Q_BLOCK
    qb = q.reshape(B, nb, Q_BLOCK, H, D).swapaxes(0, 1)
    cb = c.reshape(B, nb, Q_BLOCK, H).swapaxes(0, 1)
    ck = c.transpose(0, 2, 1)
    kpos = jnp.arange(S)

    def blk(args):
        qi, ci, i = args
        s = (jnp.einsum('bqhd,bkhd->bhqk', qi, k).astype(jnp.float32) * FOX_SCALE
             + ci.transpose(0, 2, 1)[..., None] - ck[:, :, None, :])
        qpos = i * Q_BLOCK + jnp.arange(Q_BLOCK)
        s = jnp.where(kpos[None, :] <= qpos[:, None], s, NEG_INF)
        p = jax.nn.softmax(s, axis=-1)
        return jnp.einsum('bhqk,bkhd->bqhd', p.astype(v.dtype), v)

    o = lax.map(blk, (qb, cb, jnp.arange(nb)))
    return o.swapaxes(0, 1).reshape(B, S, H, D)


def _fox_sample(q, k, v, log_f, pool_k, pool_v, pool_l, j, page_table):
    T = q.shape[1]

    def one(args):
        qi, ki, vi, li, pages = args
        kc = jnp.concatenate([pool_k[j, pages].reshape(-1, H_C, HD_C).astype(ki.dtype), ki], axis=0)
        vc = jnp.concatenate([pool_v[j, pages].reshape(-1, H_C, HD_C).astype(vi.dtype), vi], axis=0)
        lc = jnp.concatenate([pool_l[j, pages].reshape(-1, H_C).astype(jnp.float32), li], axis=0)
        c = jnp.cumsum(lc, axis=0)
        past = kc.shape[0] - T
        cq = c[past:]
        s = (jnp.einsum('thd,khd->htk', qi, kc).astype(jnp.float32) * FOX_SCALE
             + cq.T[:, :, None] - c.T[:, None, :])
        mask = jnp.arange(past + T)[None, :] <= past + jnp.arange(T)[:, None]
        p = jax.nn.softmax(jnp.where(mask, s, NEG_INF), axis=-1)
        return jnp.einsum('htk,khd->thd', p.astype(vc.dtype), vc)

    return lax.map(one, (q, k, v, log_f, page_table))


def _even_mixer(h, pos, j, W, past):
    B, T, _ = h.shape
    z = h @ W['w_in0'][j]
    q_r, k_r, v_r, g_r, c_q, c_kv, k_pe = _split(z, IN0_WIDTHS)
    log_gamma = jnp.log1p(-jnp.exp2(-5.0 - jnp.arange(H_A, dtype=jnp.float32)))
    q_r = _rope(q_r.reshape(B, T, H_A, DK_A), pos)
    k_r = _rope(k_r.reshape(B, T, H_A, DK_A), pos) * (DK_A ** -0.5)
    v_r = v_r.reshape(B, T, H_A, DV_A)
    if past is None:
        o_r, s_ret = _retention_prompt(q_r, k_r, v_r, log_gamma)
    else:
        o_r, s_ret = _retention_chunk(q_r, k_r, v_r, past['state_ret'][j].astype(q_r.dtype), log_gamma)
    o_r = jax.nn.silu(g_r) * _head_norm(o_r, W['ret_gn'][j]).reshape(B, T, H_A * DV_A)
    c_q = _rmsnorm(c_q, W['mla_q_norm'][j])
    q = jnp.einsum('btr,rhd->bthd', c_q, W['mla_w_uq'][j])
    q_nope = q[..., :NOPE_B]
    q_pe = _rope(q[..., NOPE_B:], pos)
    c_kv = _rmsnorm(c_kv, W['mla_kv_norm'][j])
    k_pe = _rope(k_pe[:, :, None, :], pos)[:, :, 0, :]
    q_lat = jnp.einsum('bthd,rhd->bthr', q_nope, W['mla_w_uk'][j])
    if past is None:
        o_lat = _mla_prompt(q_lat, q_pe, c_kv, k_pe)
    else:
        o_lat = _mla_sample(q_lat, q_pe, c_kv, k_pe, past['cache_mla_ckv'], past['cache_mla_kpe'], j,
                            past['page_table'])
    o_m = jnp.einsum('bthr,rhe->bthe', o_lat, W['mla_w_uv'][j]).reshape(B, T, H_B * VD_B)
    y = jnp.concatenate([o_r, o_m], axis=-1) @ W['w_out0'][j]
    return y, (s_ret, c_kv, k_pe)


def _odd_mixer(h, j, W, past):
    B, T, _ = h.shape
    z = h @ W['w_in1'][j]
    q, k, v, f_logit = _split(z, IN1_WIDTHS)
    q = q.reshape(B, T, H_C, HD_C)
    k = k.reshape(B, T, H_C, HD_C)
    v = v.reshape(B, T, H_C, HD_C)
    log_f = jax.nn.log_sigmoid((f_logit + W['fox_b_f'][j]).astype(jnp.float32))
    if past is None:
        o = _fox_prompt(q, k, v, log_f)
    else:
        o = _fox_sample(q, k, v, log_f, past['cache_fox_k'], past['cache_fox_v'], past['cache_fox_logf'], j,
                        past['page_table'])
    y = o.reshape(B, T, H_C * HD_C) @ W['w_out1'][j]
    return y, (k, v, log_f)


def _trunk(x, p, pos, past, W):
    rets, ckvs, kpes, fks, fvs, fls = [], [], [], [], [], []
    for i in range(DEPTH):
        g = W['norm_gains'][i]
        x = x + 0.5 * _rmsnorm(_swiglu(_rmsnorm(x, g[0]), W['ffn_w_in'][i, 0], W['ffn_w_out'][i, 0]), g[1])
        h = _rmsnorm(x, g[2])
        j = i // 2
        if i % 2 == 0:
            y, (s_ret, c_kv, k_pe) = _even_mixer(h, pos, j, W, past)
            rets.append(s_ret)
            ckvs.append(c_kv)
            kpes.append(k_pe)
        else:
            y, (k, v, log_f) = _odd_mixer(h, j, W, past)
            fks.append(k)
            fvs.append(v)
            fls.append(log_f)
        x = x + _rmsnorm(y, g[3])
        x = x + 0.5 * _rmsnorm(_swiglu(_rmsnorm(x, g[4]), W['ffn_w_in'][i, 1], W['ffn_w_out'][i, 1]), g[5])
        gate = jax.nn.sigmoid(_rmsnorm(x, g[6]) @ W['ple_w_gate'][i])
        x = x + _rmsnorm(gate * (p[i] @ W['ple_w_proj'][i]), g[7])
    return (x, jnp.stack(rets), jnp.stack(ckvs), jnp.stack(kpes),
            jnp.stack(fks), jnp.stack(fvs), jnp.stack(fls))


def setup_inputs(seed: int = 0) -> dict:
    key = jax.random.key(seed)
    ks = jax.random.split(key, 32)
    f32 = jnp.float32

    def nrm(k, shape, scale=1.0):
        return jax.random.normal(k, shape, f32) * scale

    n_pages = PAST_LEN // PAGE_SIZE
    n_used = DEC_BATCH * n_pages
    n_pool = n_used + n_used // 4
    page_table = jax.random.permutation(ks[0], n_pool)[:n_used].reshape(DEC_BATCH, n_pages).astype(jnp.int32)
    return {
        'x_prompt': nrm(ks[1], (BATCH, SEQ, D_MODEL)),
        'x_sample': nrm(ks[2], (DEC_BATCH, DEC_SEQ, D_MODEL)),
        'state_ret': nrm(ks[3], (N_EVEN, DEC_BATCH, H_A, DK_A, DV_A)),
        'cache_mla_ckv': nrm(ks[4], (N_EVEN, n_pool, PAGE_SIZE, KV_RANK)),
        'cache_mla_kpe': nrm(ks[5], (N_EVEN, n_pool, PAGE_SIZE, ROPE_B)),
        'cache_fox_k': nrm(ks[6], (N_ODD, n_pool, PAGE_SIZE, H_C, HD_C)),
        'cache_fox_v': nrm(ks[7], (N_ODD, n_pool, PAGE_SIZE, H_C, HD_C)),
        'cache_fox_logf': jax.nn.log_sigmoid(FORGET_BIAS + nrm(ks[8], (N_ODD, n_pool, PAGE_SIZE, H_C), 0.5)),
        'page_table': page_table,
        'p_prompt': nrm(ks[9], (DEPTH, BATCH, SEQ, PLE_DIM)),
        'p_sample': nrm(ks[10], (DEPTH, DEC_BATCH, DEC_SEQ, PLE_DIM)),
        'norm_gains': 1.0 + nrm(ks[11], (DEPTH, 8, D_MODEL), 0.05),
        'ffn_w_in': nrm(ks[12], (DEPTH, 2, D_MODEL, 2 * D_FF), D_MODEL ** -0.5),
        'ffn_w_out': nrm(ks[13], (DEPTH, 2, D_FF, D_MODEL), D_FF ** -0.5),
        'ple_w_gate': nrm(ks[14], (DEPTH, D_MODEL, D_MODEL), D_MODEL ** -0.5),
        'ple_w_proj': nrm(ks[15], (DEPTH, PLE_DIM, D_MODEL), PLE_DIM ** -0.5),
        'w_in0': nrm(ks[16], (N_EVEN, D_MODEL, IN0_DIM), D_MODEL ** -0.5),
        'ret_gn': 1.0 + nrm(ks[17], (N_EVEN, H_A, DV_A), 0.05),
        'mla_q_norm': 1.0 + nrm(ks[18], (N_EVEN, Q_RANK), 0.05),
        'mla_w_uq': nrm(ks[19], (N_EVEN, Q_RANK, H_B, NOPE_B + ROPE_B), Q_RANK ** -0.5),
        'mla_kv_norm': 1.0 + nrm(ks[20], (N_EVEN, KV_RANK), 0.05),
        'mla_w_uk': nrm(ks[21], (N_EVEN, KV_RANK, H_B, NOPE_B), KV_RANK ** -0.5),
        'mla_w_uv': nrm(ks[22], (N_EVEN, KV_RANK, H_B, VD_B), KV_RANK ** -0.5),
        'w_out0': nrm(ks[23], (N_EVEN, OUT0_DIM, D_MODEL), OUT0_DIM ** -0.5),
        'w_in1': nrm(ks[24], (N_ODD, D_MODEL, IN1_DIM), D_MODEL ** -0.5),
        'fox_b_f': FORGET_BIAS + nrm(ks[25], (N_ODD, H_C), 0.5),
        'w_out1': nrm(ks[26], (N_ODD, H_C * HD_C, D_MODEL), (H_C * HD_C) ** -0.5),
    }


def reference(x_prompt, x_sample, state_ret, cache_mla_ckv, cache_mla_kpe, cache_fox_k, cache_fox_v,
              cache_fox_logf, page_table, p_prompt, p_sample, norm_gains, ffn_w_in, ffn_w_out, ple_w_gate,
              ple_w_proj, w_in0, ret_gn, mla_q_norm, mla_w_uq, mla_kv_norm, mla_w_uk, mla_w_uv, w_out0,
              w_in1, fox_b_f, w_out1):
    W = dict(norm_gains=norm_gains, ffn_w_in=ffn_w_in, ffn_w_out=ffn_w_out, ple_w_gate=ple_w_gate,
             ple_w_proj=ple_w_proj, w_in0=w_in0, ret_gn=ret_gn, mla_q_norm=mla_q_norm, mla_w_uq=mla_w_uq,
             mla_kv_norm=mla_kv_norm, mla_w_uk=mla_w_uk, mla_w_uv=mla_w_uv, w_out0=w_out0, w_in1=w_in1,
             fox_b_f=fox_b_f, w_out1=w_out1)
    past = dict(state_ret=state_ret, cache_mla_ckv=cache_mla_ckv, cache_mla_kpe=cache_mla_kpe,
                cache_fox_k=cache_fox_k, cache_fox_v=cache_fox_v, cache_fox_logf=cache_fox_logf,
                page_table=page_table)
    pos_prompt = jnp.arange(x_prompt.shape[1], dtype=jnp.int32)
    pos_sample = PAST_LEN + jnp.arange(x_sample.shape[1], dtype=jnp.int32)
    y_prompt, ret_p, ckv_p, kpe_p, fk_p, fv_p, fl_p = _trunk(x_prompt, p_prompt, pos_prompt, None, W)
    y_sample, ret_s, ckv_s, kpe_s, fk_s, fv_s, fl_s = _trunk(x_sample, p_sample, pos_sample, past, W)
    return (y_prompt, y_sample, ret_p, ret_s, ckv_p, ckv_s, kpe_p, kpe_s, fk_p, fk_s, fv_p, fv_s, fl_p, fl_s)
```

```python
import functools

import jax
import jax.numpy as jnp
import numpy as np
from jax import lax
from jax.experimental import pallas as pl
from jax.experimental.pallas import tpu as pltpu

F32 = jnp.float32
BF16 = jnp.bfloat16

H_A, DK_A, DV_A = 4, 128, 128
RET_CHUNK = 128
H_B, NOPE_B, ROPE_B, VD_B = 8, 64, 32, 64
Q_RANK, KV_RANK = 256, 128
H_C, HD_C = 16, 64
ROPE_THETA = 10000.0
NORM_EPS = 1e-6
NEG_INF = -1e30
MLA_SCALE = (NOPE_B + ROPE_B) ** -0.5
FOX_SCALE = HD_C ** -0.5

LANES = 128
IN0_PAD = 2560
IN1_PAD = 3200
VMEM_LIMIT = 56 * 1024 * 1024

_NT = (((1,), (1,)), ((), ()))
_TN = (((0,), (0,)), ((), ()))


def _dot(a, b):
    return jnp.dot(a, b, preferred_element_type=F32)


def _dot_nt(a, b):
    return lax.dot_general(a, b, _NT, preferred_element_type=F32)


def _dot_tn(a, b):
    return lax.dot_general(a, b, _TN, preferred_element_type=F32)


def _rms(x, g):
    return x * lax.rsqrt(jnp.mean(x * x, axis=-1, keepdims=True) + NORM_EPS) * g


def _row_tile(n, cap=512):
    t = cap
    while n % t:
        t //= 2
    assert t >= 8, n
    return t


def _params(*sem):
    return pltpu.CompilerParams(dimension_semantics=sem, vmem_limit_bytes=VMEM_LIMIT)


def _resident(shape):
    nd = len(shape)
    return pl.BlockSpec(shape, lambda *_: (0,) * nd, pipeline_mode=pl.Buffered(1))


def _ffn_body(x, g_pre, g_post, win_ref, wout_ref, dff, chunk):
    h = _rms(x, g_pre).astype(BF16)
    acc = jnp.zeros(x.shape, F32)
    for c0 in range(0, dff, chunk):
        a = _dot(h, win_ref[:, c0:c0 + chunk])
        b = _dot(h, win_ref[:, dff + c0:dff + c0 + chunk])
        act = (a * jax.nn.sigmoid(a)) * b
        acc = acc + _dot(act.astype(BF16), wout_ref[c0:c0 + chunk, :])
    return x + 0.5 * _rms(acc, g_post)


def _ffn_kernel(x_ref, g_ref, win_ref, wout_ref, o_ref, *, dff, chunk, gi):
    o_ref[...] = _ffn_body(x_ref[...], g_ref[gi:gi + 1], g_ref[gi + 1:gi + 2], win_ref, wout_ref, dff, chunk)


def _ffn(x, gains, w_in, w_out, gi):
    n, d = x.shape
    dff = w_out.shape[0]
    chunk = 256 if dff % 256 == 0 else LANES
    tm = _row_tile(n)
    return pl.pallas_call(
        functools.partial(_ffn_kernel, dff=dff, chunk=chunk, gi=gi),
        grid=(n // tm,),
        in_specs=[pl.BlockSpec((tm, d), lambda i: (i, 0)), _resident(gains.shape),
                  _resident(w_in.shape), _resident(w_out.shape)],
        out_specs=pl.BlockSpec((tm, d), lambda i: (i, 0)),
        out_shape=jax.ShapeDtypeStruct((n, d), F32),
        compiler_params=_params("parallel"),
        name="ffn",
    )(x, gains, w_in, w_out)


def _ple_kernel(x_ref, p_ref, g_ref, wg_ref, wp_ref, o_ref):
    x = x_ref[...]
    gate = jax.nn.sigmoid(_dot(_rms(x, g_ref[6:7]).astype(BF16), wg_ref[...]))
    e = _dot(p_ref[...].astype(BF16), wp_ref[...])
    o_ref[...] = x + _rms(gate * e, g_ref[7:8])


def _ple(x, p, gains, w_gate, w_proj):
    n, d = x.shape
    tm = _row_tile(n)
    return pl.pallas_call(
        _ple_kernel,
        grid=(n // tm,),
        in_specs=[pl.BlockSpec((tm, d), lambda i: (i, 0)), pl.BlockSpec((tm, p.shape[1]), lambda i: (i, 0)),
                  _resident(gains.shape), _resident(w_gate.shape), _resident(w_proj.shape)],
        out_specs=pl.BlockSpec((tm, d), lambda i: (i, 0)),
        out_shape=jax.ShapeDtypeStruct((n, d), F32),
        compiler_params=_params("parallel"),
        name="ple",
    )(x, p, gains, w_gate, w_proj)


def _even_in_kernel(x_ref, g_ref, win_ref, ra_ref, rb_ref, qn_ref, kvn_ref, wuq_ref, wbd_ref,
                    ret_ref, qc_ref, kc_ref, ckv_ref, kpe_ref):
    h = _rms(x_ref[...], g_ref[2:3]).astype(BF16)
    z = _dot(h, win_ref[...])
    hw = H_A * DK_A
    cosr, sinr = ra_ref[:, 0:LANES], ra_ref[:, LANES:2 * LANES]
    for hh in range(H_A):
        for base, sc in ((0, None), (hw, DK_A ** -0.5)):
            xh = z[:, base + hh * DK_A:base + (hh + 1) * DK_A]
            r = xh * cosr + pltpu.roll(xh, DK_A // 2, 1) * sinr
            ret_ref[:, base + hh * DK_A:base + (hh + 1) * DK_A] = r if sc is None else r * sc
    ret_ref[:, 2 * hw:4 * hw] = z[:, 2 * hw:4 * hw]

    c_cos, s_lo, s_hi = rb_ref[:, 0:LANES], rb_ref[:, LANES:2 * LANES], rb_ref[:, 2 * LANES:3 * LANES]

    def rope32(v):
        return (v * c_cos + pltpu.roll(v, LANES - ROPE_B // 2, 1) * s_lo
                + pltpu.roll(v, ROPE_B // 2, 1) * s_hi)

    o = 4 * hw
    cq = _rms(z[:, o:o + Q_RANK], qn_ref[...]).astype(BF16)
    q = _dot(cq, wuq_ref[...])
    nq = H_B * NOPE_B
    qlat = _dot(q[:, :nq].astype(BF16), wbd_ref[...])
    ckv = _rms(z[:, o + Q_RANK:o + Q_RANK + KV_RANK], kvn_ref[...])
    kpe = rope32(z[:, o + Q_RANK + KV_RANK:o + Q_RANK + KV_RANK + LANES])
    ckv_ref[...] = ckv
    kpe_ref[...] = kpe[:, :ROPE_B]
    kc_ref[:, 0:KV_RANK] = ckv.astype(BF16)
    kc_ref[:, KV_RANK:KV_RANK + LANES] = kpe.astype(BF16)
    for hh in range(H_B):
        qc_ref[hh, :, 0:KV_RANK] = qlat[:, hh * KV_RANK:(hh + 1) * KV_RANK].astype(BF16)
        qc_ref[hh, :, KV_RANK:KV_RANK + LANES] = rope32(q[:, nq + hh * LANES:nq + (hh + 1) * LANES]).astype(BF16)


def _even_in(x, gains, w_in, rope_a, rope_b, qn, kvn, w_uq, w_bd):
    n, d = x.shape
    tm = _row_tile(n)
    kw = KV_RANK + LANES
    row = lambda w: pl.BlockSpec((tm, w), lambda i: (i, 0))
    return pl.pallas_call(
        _even_in_kernel,
        grid=(n // tm,),
        in_specs=[row(d), _resident(gains.shape), _resident(w_in.shape), row(rope_a.shape[1]),
                  row(rope_b.shape[1]), _resident(qn.shape), _resident(kvn.shape), _resident(w_uq.shape),
                  _resident(w_bd.shape)],
        out_specs=[row(4 * H_A * DK_A), pl.BlockSpec((H_B, tm, kw), lambda i: (0, i, 0)), row(kw),
                   row(KV_RANK), row(ROPE_B)],
        out_shape=[jax.ShapeDtypeStruct((n, 4 * H_A * DK_A), F32), jax.ShapeDtypeStruct((H_B, n, kw), BF16),
                   jax.ShapeDtypeStruct((n, kw), BF16), jax.ShapeDtypeStruct((n, KV_RANK), F32),
                   jax.ShapeDtypeStruct((n, ROPE_B), F32)],
        compiler_params=_params("parallel"),
        name="even_in",
    )(x, gains, w_in, rope_a, rope_b, qn, kvn, w_uq, w_bd)


def _even_out_kernel(x_ref, or_ref, gr_ref, ol_ref, g_ref, gn_ref, wuv_ref, wout_ref, o_ref):
    hw = H_A * DV_A
    gate = gr_ref[...]
    gate = gate * jax.nn.sigmoid(gate)
    y = None
    for hh in range(H_A):
        sl = slice(hh * DV_A, (hh + 1) * DV_A)
        oh = or_ref[:, sl]
        xc = oh - jnp.mean(oh, axis=-1, keepdims=True)
        var = jnp.mean(xc * xc, axis=-1, keepdims=True)
        a = gate[:, sl] * (xc * lax.rsqrt(var + NORM_EPS) * gn_ref[:, sl])
        t = _dot(a.astype(BF16), wout_ref[sl, :])
        y = t if y is None else y + t
    om = None
    for hh in range(H_B):
        t = _dot(ol_ref[hh], wuv_ref[hh])
        om = t if om is None else om + t
    y = y + _dot(om.astype(BF16), wout_ref[hw:, :])
    o_ref[...] = x_ref[...] + _rms(y, g_ref[3:4])


def _even_out(x, o_r, ret, o_lat, gains, gn, w_uv, w_out):
    n, d = x.shape
    tm = _row_tile(n)
    hw = H_A * DV_A
    return pl.pallas_call(
        _even_out_kernel,
        grid=(n // tm,),
        in_specs=[pl.BlockSpec((tm, d), lambda i: (i, 0)), pl.BlockSpec((tm, hw), lambda i: (i, 0)),
                  pl.BlockSpec((tm, hw), lambda i: (i, 3)),
                  pl.BlockSpec((H_B, tm, KV_RANK), lambda i: (0, i, 0)),
                  _resident(gains.shape), _resident(gn.shape), _resident(w_uv.shape), _resident(w_out.shape)],
        out_specs=pl.BlockSpec((tm, d), lambda i: (i, 0)),
        out_shape=jax.ShapeDtypeStruct((n, d), F32),
        compiler_params=_params("parallel"),
        name="even_out",
    )(x, o_r, ret, o_lat, gains, gn, w_uv, w_out)


def _odd_in_kernel(x_ref, g_ref, win_ref, bf_ref, qkv_ref, fk_ref, fv_ref, fl_ref):
    h = _rms(x_ref[...], g_ref[2:3]).astype(BF16)
    z = _dot(h, win_ref[...])
    w = H_C * HD_C
    qkv_ref[:, 0:w] = (z[:, 0:w] * FOX_SCALE).astype(BF16)
    qkv_ref[:, w:3 * w] = z[:, w:3 * w].astype(BF16)
    fk_ref[...] = z[:, w:2 * w]
    fv_ref[...] = z[:, 2 * w:3 * w]
    t = z[:, 3 * w:3 * w + LANES] + bf_ref[...]
    ls = jnp.minimum(t, 0.0) - jnp.log1p(jnp.exp(-jnp.abs(t)))
    fl_ref[...] = ls[:, :H_C]


def _odd_in(x, gains, w_in, b_f):
    n, d = x.shape
    tm = _row_tile(n)
    w = H_C * HD_C
    row = lambda c: pl.BlockSpec((tm, c), lambda i: (i, 0))
    return pl.pallas_call(
        _odd_in_kernel,
        grid=(n // tm,),
        in_specs=[row(d), _resident(gains.shape), _resident(w_in.shape), _resident(b_f.shape)],
        out_specs=[row(3 * w), row(w), row(w), row(H_C)],
        out_shape=[jax.ShapeDtypeStruct((n, 3 * w), BF16), jax.ShapeDtypeStruct((n, w), F32),
                   jax.ShapeDtypeStruct((n, w), F32), jax.ShapeDtypeStruct((n, H_C), F32)],
        compiler_params=_params("parallel"),
        name="odd_in",
    )(x, gains, w_in, b_f)


def _odd_out_kernel(x_ref, a_ref, g_ref, wout_ref, o_ref):
    o_ref[...] = x_ref[...] + _rms(_dot(a_ref[...], wout_ref[...]), g_ref[3:4])


def _odd_out(x, a, gains, w_out):
    n, d = x.shape
    tm = _row_tile(n)
    return pl.pallas_call(
        _odd_out_kernel,
        grid=(n // tm,),
        in_specs=[pl.BlockSpec((tm, d), lambda i: (i, 0)), pl.BlockSpec((tm, a.shape[1]), lambda i: (i, 0)),
                  _resident(gains.shape), _resident(w_out.shape)],
        out_specs=pl.BlockSpec((tm, d), lambda i: (i, 0)),
        out_shape=jax.ShapeDtypeStruct((n, d), F32),
        compiler_params=_params("parallel"),
        name="odd_out",
    )(x, a, gains, w_out)


def _ret_kernel(qkv_ref, s0_ref, dec_ref, qd_ref, kd_ref, sd_ref, o_ref, s_ref):
    @pl.when(pl.program_id(1) == 0)
    def _():
        s_ref[...] = s0_ref[...]

    for hh in range(H_A):
        q = qkv_ref[:, hh * DK_A:(hh + 1) * DK_A]
        k = qkv_ref[:, (H_A + hh) * DK_A:(H_A + hh + 1) * DK_A]
        v = qkv_ref[:, 2 * H_A * DK_A + hh * DV_A:2 * H_A * DK_A + (hh + 1) * DV_A].astype(BF16)
        s = s_ref[0, hh]
        qb = q.astype(BF16)
        scores = _dot_nt(qb, k.astype(BF16)) * dec_ref[hh]
        o = _dot(scores.astype(BF16), v) + _dot(qb, s.astype(BF16)) * qd_ref[hh]
        o_ref[:, hh * DV_A:(hh + 1) * DV_A] = o
        s_ref[0, hh] = sd_ref[hh] * s + _dot_tn((k * kd_ref[hh]).astype(BF16), v)


def _retention(qkv, s0, nb, nc, length):
    lp = RET_CHUNK
    log_gamma = jnp.log1p(-jnp.exp2(-5.0 - jnp.arange(H_A, dtype=F32)))
    idx = jnp.arange(lp, dtype=F32)
    diff = idx[:, None] - idx[None, :]
    decay = jnp.where(diff >= 0, jnp.exp(log_gamma[:, None, None] * jnp.maximum(diff, 0.0)), 0.0)
    ones = jnp.ones((1, 1, DV_A), F32)
    qd = jnp.exp(log_gamma[:, None] * (idx[None, :] + 1.0))[:, :, None] * ones
    kd = jnp.exp(log_gamma[:, None] * (length - 1.0 - idx[None, :]))[:, :, None] * ones
    sd = jnp.exp(log_gamma * length)[:, None, None] * ones
    wq = 3 * H_A * DK_A
    return pl.pallas_call(
        _ret_kernel,
        grid=(nb, nc),
        in_specs=[pl.BlockSpec((lp, wq), lambda b, c: (b * nc + c, 0)),
                  pl.BlockSpec((1, H_A, DK_A, DV_A), lambda b, c: (b, 0, 0, 0)),
                  _resident(decay.shape), _resident(qd.shape), _resident(kd.shape), _resident(sd.shape)],
        out_specs=[pl.BlockSpec((lp, H_A * DV_A), lambda b, c: (b * nc + c, 0)),
                   pl.BlockSpec((1, H_A, DK_A, DV_A), lambda b, c: (b, 0, 0, 0))],
        out_shape=[jax.ShapeDtypeStruct((nb * nc * lp, H_A * DV_A), F32),
                   jax.ShapeDtypeStruct((nb, H_A, DK_A, DV_A), F32)],
        compiler_params=_params("parallel", "arbitrary"),
        name="retention",
    )(qkv, s0, decay, qd, kd, sd)


def _online_update(m_ref, l_ref, acc_ref, s, pv):
    m_prev = m_ref[...]
    m_new = jnp.maximum(m_prev, jnp.max(s, axis=1, keepdims=True))
    alpha = jnp.exp(m_prev - m_new)
    p = jnp.exp(s - m_new)
    l_ref[...] = alpha * l_ref[...] + jnp.sum(p, axis=1, keepdims=True)
    acc_ref[...] = alpha * acc_ref[...] + pv(p.astype(BF16))
    m_ref[...] = m_new


def _init_state(m_ref, l_ref, acc_ref):
    m_ref[...] = jnp.full(m_ref.shape, NEG_INF, F32)
    l_ref[...] = jnp.zeros(l_ref.shape, F32)
    acc_ref[...] = jnp.zeros(acc_ref.shape, F32)


def _mla_flash_kernel(q_ref, k_ref, o_ref, m_ref, l_ref, acc_ref, *, tq, tk):
    i = pl.program_id(1)
    rows = H_B * tq
    q = q_ref[...].reshape(rows, q_ref.shape[2])
    _init_state(m_ref, l_ref, acc_ref)

    def step(j, masked):
        off = pl.multiple_of(j * tk, tk)
        k = k_ref[pl.ds(off, tk), :]
        s = _dot_nt(q, k) * MLA_SCALE
        if masked:
            qpos = i * tq + (lax.broadcasted_iota(jnp.int32, (rows, tk), 0) & (tq - 1))
            kpos = off + lax.broadcasted_iota(jnp.int32, (rows, tk), 1)
            s = jnp.where(kpos <= qpos, s, NEG_INF)
        _online_update(m_ref, l_ref, acc_ref, s, lambda p: _dot(p, k[:, :KV_RANK]))

    nfull = (i * tq) // tk

    def body(j, c):
        step(j, False)
        return c

    lax.fori_loop(0, nfull, body, 0)
    step(nfull, True)
    o = acc_ref[...] / l_ref[...]
    o_ref[...] = o.reshape(H_B, tq, KV_RANK).astype(BF16)


def _mla_prompt(qc, kc, nb, seq):
    tq = min(128, seq)
    tk = min(512, seq)
    nq = seq // tq
    kw = kc.shape[1]
    return pl.pallas_call(
        functools.partial(_mla_flash_kernel, tq=tq, tk=tk),
        grid=(nb, nq),
        in_specs=[pl.BlockSpec((H_B, tq, kw), lambda b, i: (0, b * nq + i, 0)),
                  pl.BlockSpec((seq, kw), lambda b, i: (b, 0))],
        out_specs=pl.BlockSpec((H_B, tq, KV_RANK), lambda b, i: (0, b * nq + i, 0)),
        out_shape=jax.ShapeDtypeStruct((H_B, nb * seq, KV_RANK), BF16),
        scratch_shapes=[pltpu.VMEM((H_B * tq, 1), F32), pltpu.VMEM((H_B * tq, 1), F32),
                        pltpu.VMEM((H_B * tq, KV_RANK), F32)],
        compiler_params=_params("parallel", "arbitrary"),
        name="mla_prompt",
    )(qc, kc)


def _cumsum_lanes(x, tri):
    hi = x.astype(BF16)
    r = x - hi.astype(F32)
    mid = r.astype(BF16)
    lo = (r - mid.astype(F32)).astype(BF16)
    cc = _dot(jnp.concatenate([hi, mid, lo], axis=0), tri)
    n = x.shape[0]
    return cc[0:n] + cc[n:2 * n] + cc[2 * n:3 * n]


def _cumsum_kernel(x_ref, tri_ref, o_ref, c_ref):
    @pl.when(pl.program_id(1) == 0)
    def _():
        c_ref[...] = jnp.zeros(c_ref.shape, F32)

    c = _cumsum_lanes(x_ref[0], tri_ref[...]) + c_ref[...]
    o_ref[0] = c
    c_ref[...] = jnp.broadcast_to(c[:, LANES - 1:LANES], c_ref.shape)


def _cumsum_seq(x_t, tri):
    nb, h, seq = x_t.shape
    return pl.pallas_call(
        _cumsum_kernel,
        grid=(nb, seq // LANES),
        in_specs=[pl.BlockSpec((1, h, LANES), lambda b, j: (b, 0, j)), _resident(tri.shape)],
        out_specs=pl.BlockSpec((1, h, LANES), lambda b, j: (b, 0, j)),
        out_shape=jax.ShapeDtypeStruct((nb, h, seq), F32),
        scratch_shapes=[pltpu.VMEM((h, LANES), F32)],
        compiler_params=_params("parallel", "arbitrary"),
        name="cumsum",
    )(x_t, tri)


def _fox_flash_kernel(q_ref, k_ref, v_ref, c_ref, o_ref, ma, la, acca, mb, lb, accb, *, tq, tk):
    i = pl.program_id(2)
    q2 = q_ref[...]
    lane = lax.broadcasted_iota(jnp.int32, q2.shape, 1)
    zero = jnp.zeros_like(q2)
    qa = jnp.where(lane < HD_C, q2, zero)
    qb = jnp.where(lane >= HD_C, q2, zero)
    _init_state(ma, la, acca)
    _init_state(mb, lb, accb)

    def step(j, masked):
        off = pl.multiple_of(j * tk, tk)
        k = k_ref[pl.ds(off, tk), :]
        v = v_ref[pl.ds(off, tk), :]
        cb = c_ref[0, :, pl.ds(off, tk)]
        if masked:
            qpos = i * tq + lax.broadcasted_iota(jnp.int32, (tq, tk), 0)
            kpos = off + lax.broadcasted_iota(jnp.int32, (tq, tk), 1)
            keep = kpos <= qpos
        for qh, r, st in ((qa, 0, (ma, la, acca)), (qb, 1, (mb, lb, accb))):
            s = _dot_nt(qh, k) - cb[r:r + 1, :]
            if masked:
                s = jnp.where(keep, s, NEG_INF)
            _online_update(*st, s, lambda p: _dot(p, v))

    nfull = (i * tq) // tk

    def body(j, c):
        step(j, False)
        return c

    lax.fori_loop(0, nfull, body, 0)
    step(nfull, True)
    lane_o = lax.broadcasted_iota(jnp.int32, (tq, LANES), 1)
    o = jnp.where(lane_o < HD_C, acca[...] / la[...], accb[...] / lb[...])
    o_ref[...] = o.astype(BF16)


def _fox_prompt(qkv, c_pairs, nb, seq):
    tq = min(256, seq)
    tk = min(512, seq)
    nq = seq // tq
    npair = H_C * HD_C // LANES
    st = lambda w: pltpu.VMEM((tq, w), F32)
    return pl.pallas_call(
        functools.partial(_fox_flash_kernel, tq=tq, tk=tk),
        grid=(nb, npair, nq),
        in_specs=[pl.BlockSpec((tq, LANES), lambda b, h, i: (b * nq + i, h)),
                  pl.BlockSpec((seq, LANES), lambda b, h, i: (b, npair + h)),
                  pl.BlockSpec((seq, LANES), lambda b, h, i: (b, 2 * npair + h)),
                  pl.BlockSpec((1, 2, seq), lambda b, h, i: (b * npair + h, 0, 0))],
        out_specs=pl.BlockSpec((tq, LANES), lambda b, h, i: (b * nq + i, h)),
        out_shape=jax.ShapeDtypeStruct((nb * seq, H_C * HD_C), BF16),
        scratch_shapes=[st(1), st(1), st(LANES), st(1), st(1), st(LANES)],
        compiler_params=_params("parallel", "parallel", "arbitrary"),
        name="fox_prompt",
    )(qkv, qkv, qkv, c_pairs)


def _mla_dec_kernel(pt_ref, q_ref, *refs, pp, t_new):
    del pt_ref
    ck_refs, kp_refs = refs[:pp], refs[pp:2 * pp]
    kn_ref, o_ref, m_ref, l_ref, acc_ref = refs[2 * pp:]
    c = pl.program_id(1)

    @pl.when(c == 0)
    def _():
        _init_state(m_ref, l_ref, acc_ref)

    q = q_ref[0]
    ql, qp = q[:, :KV_RANK], q[:, KV_RANK:KV_RANK + ROPE_B]
    cks = [r[0, 0].astype(BF16) for r in ck_refs]
    s = jnp.concatenate([_dot_nt(ql, ck) + _dot_nt(qp, kp[0, 0].astype(BF16))
                         for ck, kp in zip(cks, kp_refs)], axis=1) * MLA_SCALE

    def pv(p):
        out = None
        for n, ck in enumerate(cks):
            t = _dot(p[:, n * LANES:(n + 1) * LANES], ck)
            out = t if out is None else out + t
        return out

    _online_update(m_ref, l_ref, acc_ref, s, pv)

    @pl.when(c == pl.num_programs(1) - 1)
    def _():
        kn = kn_ref[0]
        sn = _dot_nt(q, kn) * MLA_SCALE
        tok = lax.broadcasted_iota(jnp.int32, sn.shape, 0) & (t_new - 1)
        col = lax.broadcasted_iota(jnp.int32, sn.shape, 1)
        sn = jnp.where(col <= tok, sn, NEG_INF)
        _online_update(m_ref, l_ref, acc_ref, sn, lambda p: _dot(p, kn[:, :KV_RANK]))
        o_ref[0] = acc_ref[...] / l_ref[...]


def _mla_decode(page_table, q, pool_ckv, pool_kpe, k_new, layer, t_new):
    bs, rows, kw = q.shape
    n_pages = page_table.shape[1]
    page = pool_ckv.shape[2]
    pp = min(16, n_pages)
    assert page == LANES and n_pages % pp == 0 and t_new & (t_new - 1) == 0
    nch = n_pages // pp
    pt = page_table.reshape(-1)

    def pool_spec(width, n):
        return pl.BlockSpec((1, 1, page, width),
                            lambda b, c, pt_ref: (layer, pt_ref[b * n_pages + c * pp + n], 0, 0))

    in_specs = ([pl.BlockSpec((1, rows, kw), lambda b, c, pt_ref: (b, 0, 0))]
                + [pool_spec(KV_RANK, n) for n in range(pp)] + [pool_spec(ROPE_B, n) for n in range(pp)]
                + [pl.BlockSpec((1, page, kw), lambda b, c, pt_ref: (b, 0, 0))])
    return pl.pallas_call(
        functools.partial(_mla_dec_kernel, pp=pp, t_new=t_new),
        grid_spec=pltpu.PrefetchScalarGridSpec(
            num_scalar_prefetch=1, grid=(bs, nch), in_specs=in_specs,
            out_specs=pl.BlockSpec((1, rows, KV_RANK), lambda b, c, pt_ref: (b, 0, 0)),
            scratch_shapes=[pltpu.VMEM((rows, 1), F32), pltpu.VMEM((rows, 1), F32),
                            pltpu.VMEM((rows, KV_RANK), F32)]),
        out_shape=jax.ShapeDtypeStruct((bs, rows, KV_RANK), F32),
        compiler_params=_params("parallel", "arbitrary"),
        name="mla_decode",
    )(pt, q, *([pool_ckv] * pp), *([pool_kpe] * pp), k_new)


def _fox_dec_kernel(pt_ref, q_ref, *refs, pp, t_new):
    del pt_ref
    k_refs, v_refs, lf_refs = refs[:pp], refs[pp:2 * pp], refs[2 * pp:3 * pp]
    kn_ref, vn_ref, lfn_ref, tri_ref, o_ref, m_ref, l_ref, acc_ref, c_ref = refs[3 * pp:]
    c = pl.program_id(1)

    @pl.when(c == 0)
    def _():
        _init_state(m_ref, l_ref, acc_ref)
        c_ref[...] = jnp.zeros(c_ref.shape, F32)

    q = q_ref[0]
    tri = tri_ref[...]

    def scores(k_bf16, lf_t):
        cpage = _cumsum_lanes(lf_t, tri) + c_ref[...]
        c_ref[...] = jnp.broadcast_to(cpage[:, LANES - 1:LANES], c_ref.shape)
        return _dot_nt(q, k_bf16) - jnp.concatenate([cpage] * t_new, axis=0)

    def pv_pages(vs):
        def pv(p):
            out = None
            for n, v in enumerate(vs):
                t = _dot(p[:, n * LANES:(n + 1) * LANES], v)
                out = t if out is None else out + t
            return out
        return pv

    s = jnp.concatenate([scores(kr[0, 0].astype(BF16), lr[0, 0]) for kr, lr in zip(k_refs, lf_refs)], axis=1)
    _online_update(m_ref, l_ref, acc_ref, s, pv_pages([r[0, 0].astype(BF16) for r in v_refs]))

    @pl.when(c == pl.num_programs(1) - 1)
    def _():
        sn = scores(kn_ref[0], lfn_ref[0])
        tok = lax.broadcasted_iota(jnp.int32, sn.shape, 0) // H_C
        col = lax.broadcasted_iota(jnp.int32, sn.shape, 1)
        sn = jnp.where(col <= tok, sn, NEG_INF)
        _online_update(m_ref, l_ref, acc_ref, sn, pv_pages([vn_ref[0]]))
        o = acc_ref[...] / l_ref[...]
        row_h = lax.broadcasted_iota(jnp.int32, o.shape, 0) % H_C
        lane_h = lax.broadcasted_iota(jnp.int32, o.shape, 1) // HD_C
        o = jnp.where(row_h == lane_h, o, 0.0)
        o_ref[0] = jnp.sum(o.reshape(t_new, H_C, o.shape[1]), axis=1)


def _fox_decode(page_table, q_bd, pool_k, pool_v, pool_lf_t, k_new, v_new, lf_new_t, tri, layer, t_new):
    bs, rows, w = q_bd.shape
    n_pages = page_table.shape[1]
    page = pool_k.shape[2]
    pp = min(8, n_pages)
    assert page == LANES and n_pages % pp == 0
    nch = n_pages // pp
    pt = page_table.reshape(-1)

    def pool_spec(shape, n):
        return pl.BlockSpec((1, 1) + shape,
                            lambda b, c, pt_ref: (layer, pt_ref[b * n_pages + c * pp + n], 0, 0))

    per_seq = lambda shape: pl.BlockSpec((1,) + shape, lambda b, c, pt_ref: (b, 0, 0))
    in_specs = ([per_seq((rows, w))]
                + [pool_spec((page, w), n) for n in range(pp)] + [pool_spec((page, w), n) for n in range(pp)]
                + [pool_spec((H_C, page), n) for n in range(pp)]
                + [per_seq((page, w)), per_seq((page, w)), per_seq((H_C, page)),
                   pl.BlockSpec(tri.shape, lambda b, c, pt_ref: (0, 0))])
    return pl.pallas_call(
        functools.partial(_fox_dec_kernel, pp=pp, t_new=t_new),
        grid_spec=pltpu.PrefetchScalarGridSpec(
            num_scalar_prefetch=1, grid=(bs, nch), in_specs=in_specs,
            out_specs=per_seq((t_new, w)),
            scratch_shapes=[pltpu.VMEM((rows, 1), F32), pltpu.VMEM((rows, 1), F32),
                            pltpu.VMEM((rows, w), F32), pltpu.VMEM((H_C, LANES), F32)]),
        out_shape=jax.ShapeDtypeStruct((bs, t_new, w), F32),
        compiler_params=_params("parallel", "arbitrary"),
        name="fox_decode",
    )(pt, q_bd, *([pool_k] * pp), *([pool_v] * pp), *([pool_lf_t] * pp), k_new, v_new, lf_new_t, tri)


def _rope_tables(pos):
    pos = pos.astype(F32)[:, None]

    def cs(half):
        inv = ROPE_THETA ** (-jnp.arange(half, dtype=F32) / half)
        ang = pos * inv[None, :]
        return jnp.cos(ang), jnp.sin(ang)

    c, s = cs(DK_A // 2)
    rope_a = jnp.concatenate([c, c, -s, s], axis=1)
    c, s = cs(ROPE_B // 2)
    z = lambda w: jnp.zeros((pos.shape[0], w), F32)
    rope_b = jnp.concatenate([c, c, z(LANES - ROPE_B), -s, z(LANES - ROPE_B // 2),
                              z(ROPE_B // 2), s, z(LANES - ROPE_B)], axis=1)
    return rope_a, rope_b


def _pad_cols(w, width):
    return jnp.pad(w, ((0, 0), (0, width - w.shape[1])))


def kernel(x_prompt, x_sample, state_ret, cache_mla_ckv, cache_mla_kpe, cache_fox_k, cache_fox_v,
           cache_fox_logf, page_table, p_prompt, p_sample, norm_gains, ffn_w_in, ffn_w_out, ple_w_gate,
           ple_w_proj, w_in0, ret_gn, mla_q_norm, mla_w_uq, mla_kv_norm, mla_w_uk, mla_w_uv, w_out0,
           w_in1, fox_b_f, w_out1):
    nb, seq, d = x_prompt.shape
    bs, t_new, _ = x_sample.shape
    depth = norm_gains.shape[0]
    n_p, n_s = nb * seq, bs * t_new
    n_pages, page = page_table.shape[1], cache_mla_ckv.shape[2]
    past_len = n_pages * page
    assert seq % RET_CHUNK == 0 and t_new <= RET_CHUNK

    x = jnp.concatenate([x_prompt.reshape(n_p, d), x_sample.reshape(n_s, d)], axis=0)
    pos = jnp.concatenate([jnp.tile(jnp.arange(seq, dtype=jnp.int32), nb),
                           jnp.tile(past_len + jnp.arange(t_new, dtype=jnp.int32), bs)])
    rope_a, rope_b = _rope_tables(pos)
    eye_b = jnp.eye(H_B, dtype=F32)
    ii = jnp.arange(LANES)
    tri = (ii[:, None] <= ii[None, :]).astype(BF16)

    rets_p, rets_s, ckvs, kpes, fks, fvs, fls = [], [], [], [], [], [], []
    for i in range(depth):
        g = norm_gains[i]
        j = i // 2
        x = _ffn(x, g, ffn_w_in[i, 0].astype(BF16), ffn_w_out[i, 0].astype(BF16), 0)
        if i % 2 == 0:
            w_in = _pad_cols(w_in0[j], IN0_PAD).astype(BF16)
            uq = mla_w_uq[j]
            w_uq = jnp.concatenate(
                [uq[:, :, :NOPE_B].reshape(Q_RANK, H_B * NOPE_B),
                 jnp.pad(uq[:, :, NOPE_B:], ((0, 0), (0, 0), (0, LANES - ROPE_B))).reshape(Q_RANK, H_B * LANES)],
                axis=1).astype(BF16)
            w_bd = jnp.einsum('hdr,hg->hdgr', jnp.transpose(mla_w_uk[j], (1, 2, 0)), eye_b)
            w_bd = w_bd.reshape(H_B * NOPE_B, H_B * KV_RANK).astype(BF16)
            w_uv = jnp.einsum('rhe,hg->hrge', mla_w_uv[j], eye_b).reshape(H_B, KV_RANK, H_B * VD_B).astype(BF16)
            ret, qc, kc, ckv, kpe = _even_in(x, g, w_in, rope_a, rope_b, mla_q_norm[j][None], mla_kv_norm[j][None],
                                             w_uq, w_bd)
            wq = 3 * H_A * DK_A
            o_rp, s_p = _retention(ret, jnp.zeros((nb, H_A, DK_A, DV_A), F32), nb, seq // RET_CHUNK, float(RET_CHUNK))
            ret_s = jnp.pad(ret[n_p:, :wq].reshape(bs, t_new, wq), ((0, 0), (0, RET_CHUNK - t_new), (0, 0)))
            o_rs, s_s = _retention(ret_s.reshape(bs * RET_CHUNK, wq), state_ret[j], bs, 1, float(t_new))
            o_rs = o_rs.reshape(bs, RET_CHUNK, H_A * DV_A)[:, :t_new].reshape(n_s, H_A * DV_A)
            o_r = jnp.concatenate([o_rp, o_rs], axis=0)
            ol_p = _mla_prompt(qc, kc, nb, seq)
            kw = kc.shape[1]
            q_s = qc[:, n_p:].reshape(H_B, bs, t_new, kw).transpose(1, 0, 2, 3).reshape(bs, H_B * t_new, kw)
            k_new = jnp.pad(kc[n_p:].reshape(bs, t_new, kw), ((0, 0), (0, page - t_new), (0, 0)))
            ol_s = _mla_decode(page_table, q_s, cache_mla_ckv, cache_mla_kpe, k_new, j, t_new)
            ol_s = ol_s.reshape(bs, H_B, t_new, KV_RANK).transpose(1, 0, 2, 3).reshape(H_B, n_s, KV_RANK)
            o_lat = jnp.concatenate([ol_p, ol_s.astype(BF16)], axis=1)
            x = _even_out(x, o_r, ret, o_lat, g, ret_gn[j].reshape(1, H_A * DV_A), w_uv, w_out0[j].astype(BF16))
            rets_p.append(s_p)
            rets_s.append(s_s)
            ckvs.append(ckv)
            kpes.append(kpe)
        else:
            w = H_C * HD_C
            w_in = _pad_cols(w_in1[j], IN1_PAD).astype(BF16)
            b_f = jnp.pad(fox_b_f[j], (0, LANES - H_C))[None]
            qkv, fk, fv, fl = _odd_in(x, g, w_in, b_f)
            c_t = _cumsum_seq(fl[:n_p].reshape(nb, seq, H_C).transpose(0, 2, 1), tri)
            o_p = _fox_prompt(qkv, c_t.reshape(nb * H_C // 2, 2, seq), nb, seq)
            q4 = qkv[n_p:, :w].reshape(bs, t_new, H_C, 1, HD_C)
            q_bd = (q4 * jnp.eye(H_C, dtype=BF16)[None, None, :, :, None]).reshape(bs, t_new * H_C, w)
            pad_rows = lambda a: jnp.pad(a.reshape(bs, t_new, w), ((0, 0), (0, page - t_new), (0, 0)))
            lf_new_t = jnp.pad(fl[n_p:].reshape(bs, t_new, H_C).transpose(0, 2, 1),
                               ((0, 0), (0, 0), (0, page - t_new)))
            n_pool = cache_fox_k.shape[1]
            o_s = _fox_decode(page_table, q_bd, cache_fox_k.reshape(-1, n_pool, page, w),
                              cache_fox_v.reshape(-1, n_pool, page, w), cache_fox_logf.transpose(0, 1, 3, 2),
                              pad_rows(qkv[n_p:, w:2 * w]), pad_rows(qkv[n_p:, 2 * w:]), lf_new_t, tri, j, t_new)
            a = jnp.concatenate([o_p, o_s.reshape(n_s, w).astype(BF16)], axis=0)
            x = _odd_out(x, a, g, w_out1[j].astype(BF16))
            fks.append(fk)
            fvs.append(fv)
            fls.append(fl)
        x = _ffn(x, g, ffn_w_in[i, 1].astype(BF16), ffn_w_out[i, 1].astype(BF16), 4)
        p = jnp.concatenate([p_prompt[i].reshape(n_p, -1), p_sample[i].reshape(n_s, -1)], axis=0)
        x = _ple(x, p, g, ple_w_gate[i].astype(BF16), ple_w_proj[i].astype(BF16))

    def split(parts, tail):
        a = jnp.stack(parts)
        return (a[:, :n_p].reshape((len(parts), nb, seq) + tail),
                a[:, n_p:].reshape((len(parts), bs, t_new) + tail))

    ckv_p, ckv_s = split(ckvs, (KV_RANK,))
    kpe_p, kpe_s = split(kpes, (ROPE_B,))
    fk_p, fk_s = split(fks, (H_C, HD_C))
    fv_p, fv_s = split(fvs, (H_C, HD_C))
    fl_p, fl_s = split(fls, (H_C,))
    return (x[:n_p].reshape(nb, seq, d), x[n_p:].reshape(bs, t_new, d), jnp.stack(rets_p), jnp.stack(rets_s),
            ckv_p, ckv_s, kpe_p, kpe_s, fk_p, fk_s, fv_p, fv_s, fl_p, fl_s)
```

```python
import functools

import jax
import jax.numpy as jnp
import numpy as np
from jax import lax
from jax.experimental import pallas as pl
from jax.experimental.pallas import tpu as pltpu

F32 = jnp.float32
BF16 = jnp.bfloat16

H_A, DK_A, DV_A = 4, 128, 128
RET_CHUNK = 128
H_B, NOPE_B, ROPE_B, VD_B = 8, 64, 32, 64
Q_RANK, KV_RANK = 256, 128
H_C, HD_C = 16, 64
ROPE_THETA = 10000.0
NORM_EPS = 1e-6
NEG_INF = -1e30
MLA_SCALE = (NOPE_B + ROPE_B) ** -0.5
FOX_SCALE = HD_C ** -0.5

LANES = 128
IN0_PAD = 2560
IN1_PAD = 3200
VMEM_LIMIT = 56 * 1024 * 1024

_NT = (((1,), (1,)), ((), ()))
_TN = (((0,), (0,)), ((), ()))


def _dot(a, b):
    return jnp.dot(a, b, preferred_element_type=F32)


def _dot_nt(a, b):
    return lax.dot_general(a, b, _NT, preferred_element_type=F32)


def _dot_tn(a, b):
    return lax.dot_general(a, b, _TN, preferred_element_type=F32)


def _rms(x, g):
    return x * lax.rsqrt(jnp.mean(x * x, axis=-1, keepdims=True) + NORM_EPS) * g


def _row_tile(n, cap=512):
    t = cap
    while n % t:
        t //= 2
    assert t >= 8, n
    return t


def _params(*sem):
    return pltpu.CompilerParams(dimension_semantics=sem, vmem_limit_bytes=VMEM_LIMIT)


def _resident(shape):
    nd = len(shape)
    return pl.BlockSpec(shape, lambda *_: (0,) * nd, pipeline_mode=pl.Buffered(1))


def _ffn_body(x, g_pre, g_post, win_ref, wout_ref, dff, chunk):
    h = _rms(x, g_pre).astype(BF16)
    acc = jnp.zeros(x.shape, F32)
    for c0 in range(0, dff, chunk):
        a = _dot(h, win_ref[:, c0:c0 + chunk])
        b = _dot(h, win_ref[:, dff + c0:dff + c0 + chunk])
        act = (a * jax.nn.sigmoid(a)) * b
        acc = acc + _dot(act.astype(BF16), wout_ref[c0:c0 + chunk, :])
    return x + 0.5 * _rms(acc, g_post)


def _ffn_kernel(x_ref, g_ref, win_ref, wout_ref, o_ref, *, dff, chunk, gi):
    o_ref[...] = _ffn_body(x_ref[...], g_ref[gi:gi + 1], g_ref[gi + 1:gi + 2], win_ref, wout_ref, dff, chunk)


def _ffn(x, gains, w_in, w_out, gi):
    n, d = x.shape
    dff = w_out.shape[0]
    chunk = 256 if dff % 256 == 0 else LANES
    tm = _row_tile(n)
    return pl.pallas_call(
        functools.partial(_ffn_kernel, dff=dff, chunk=chunk, gi=gi),
        grid=(n // tm,),
        in_specs=[pl.BlockSpec((tm, d), lambda i: (i, 0)), _resident(gains.shape),
                  _resident(w_in.shape), _resident(w_out.shape)],
        out_specs=pl.BlockSpec((tm, d), lambda i: (i, 0)),
        out_shape=jax.ShapeDtypeStruct((n, d), F32),
        compiler_params=_params("parallel"),
        name="ffn",
    )(x, gains, w_in, w_out)


def _ple_kernel(x_ref, p_ref, g_ref, wg_ref, wp_ref, o_ref):
    x = x_ref[...]
    gate = jax.nn.sigmoid(_dot(_rms(x, g_ref[6:7]).astype(BF16), wg_ref[...]))
    e = _dot(p_ref[...].astype(BF16), wp_ref[...])
    o_ref[...] = x + _rms(gate * e, g_ref[7:8])


def _ple(x, p, gains, w_gate, w_proj):
    n, d = x.shape
    tm = _row_tile(n)
    return pl.pallas_call(
        _ple_kernel,
        grid=(n // tm,),
        in_specs=[pl.BlockSpec((tm, d), lambda i: (i, 0)), pl.BlockSpec((tm, p.shape[1]), lambda i: (i, 0)),
                  _resident(gains.shape), _resident(w_gate.shape), _resident(w_proj.shape)],
        out_specs=pl.BlockSpec((tm, d), lambda i: (i, 0)),
        out_shape=jax.ShapeDtypeStruct((n, d), F32),
        compiler_params=_params("parallel"),
        name="ple",
    )(x, p, gains, w_gate, w_proj)


def _even_in_kernel(x_ref, g_ref, win_ref, ra_ref, rb_ref, qn_ref, kvn_ref, wuq_ref, wbd_ref,
                    ret_ref, qc_ref, kc_ref, ckv_ref, kpe_ref):
    h = _rms(x_ref[...], g_ref[2:3]).astype(BF16)
    z = _dot(h, win_ref[...])
    hw = H_A * DK_A
    cosr, sinr = ra_ref[:, 0:LANES], ra_ref[:, LANES:2 * LANES]
    for hh in range(H_A):
        for base, sc in ((0, None), (hw, DK_A ** -0.5)):
            xh = z[:, base + hh * DK_A:base + (hh + 1) * DK_A]
            r = xh * cosr + pltpu.roll(xh, DK_A // 2, 1) * sinr
            ret_ref[:, base + hh * DK_A:base + (hh + 1) * DK_A] = r if sc is None else r * sc
    ret_ref[:, 2 * hw:4 * hw] = z[:, 2 * hw:4 * hw]

    c_cos, s_lo, s_hi = rb_ref[:, 0:LANES], rb_ref[:, LANES:2 * LANES], rb_ref[:, 2 * LANES:3 * LANES]

    def rope32(v):
        return (v * c_cos + pltpu.roll(v, LANES - ROPE_B // 2, 1) * s_lo
                + pltpu.roll(v, ROPE_B // 2, 1) * s_hi)

    o = 4 * hw
    cq = _rms(z[:, o:o + Q_RANK], qn_ref[...]).astype(BF16)
    q = _dot(cq, wuq_ref[...])
    nq = H_B * NOPE_B
    qlat = _dot(q[:, :nq].astype(BF16), wbd_ref[...])
    ckv = _rms(z[:, o + Q_RANK:o + Q_RANK + KV_RANK], kvn_ref[...])
    kpe = rope32(z[:, o + Q_RANK + KV_RANK:o + Q_RANK + KV_RANK + LANES])
    ckv_ref[...] = ckv
    kpe_ref[...] = kpe[:, :ROPE_B]
    kc_ref[:, 0:KV_RANK] = ckv.astype(BF16)
    kc_ref[:, KV_RANK:KV_RANK + LANES] = kpe.astype(BF16)
    for hh in range(H_B):
        qc_ref[hh, :, 0:KV_RANK] = qlat[:, hh * KV_RANK:(hh + 1) * KV_RANK].astype(BF16)
        qc_ref[hh, :, KV_RANK:KV_RANK + LANES] = rope32(q[:, nq + hh * LANES:nq + (hh + 1) * LANES]).astype(BF16)


def _even_in(x, gains, w_in, rope_a, rope_b, qn, kvn, w_uq, w_bd):
    n, d = x.shape
    tm = _row_tile(n)
    kw = KV_RANK + LANES
    row = lambda w: pl.BlockSpec((tm, w), lambda i: (i, 0))
    return pl.pallas_call(
        _even_in_kernel,
        grid=(n // tm,),
        in_specs=[row(d), _resident(gains.shape), _resident(w_in.shape), row(rope_a.shape[1]),
                  row(rope_b.shape[1]), _resident(qn.shape), _resident(kvn.shape), _resident(w_uq.shape),
                  _resident(w_bd.shape)],
        out_specs=[row(4 * H_A * DK_A), pl.BlockSpec((H_B, tm, kw), lambda i: (0, i, 0)), row(kw),
                   row(KV_RANK), row(ROPE_B)],
        out_shape=[jax.ShapeDtypeStruct((n, 4 * H_A * DK_A), F32), jax.ShapeDtypeStruct((H_B, n, kw), BF16),
                   jax.ShapeDtypeStruct((n, kw), BF16), jax.ShapeDtypeStruct((n, KV_RANK), F32),
                   jax.ShapeDtypeStruct((n, ROPE_B), F32)],
        compiler_params=_params("parallel"),
        name="even_in",
    )(x, gains, w_in, rope_a, rope_b, qn, kvn, w_uq, w_bd)


def _even_out_kernel(x_ref, or_ref, gr_ref, ol_ref, g_ref, gn_ref, wuv_ref, wout_ref, o_ref):
    hw = H_A * DV_A
    gate = gr_ref[...]
    gate = gate * jax.nn.sigmoid(gate)
    y = None
    for hh in range(H_A):
        sl = slice(hh * DV_A, (hh + 1) * DV_A)
        oh = or_ref[:, sl]
        xc = oh - jnp.mean(oh, axis=-1, keepdims=True)
        var = jnp.mean(xc * xc, axis=-1, keepdims=True)
        a = gate[:, sl] * (xc * lax.rsqrt(var + NORM_EPS) * gn_ref[:, sl])
        t = _dot(a.astype(BF16), wout_ref[sl, :])
        y = t if y is None else y + t
    om = None
    for hh in range(H_B):
        t = _dot(ol_ref[hh], wuv_ref[hh])
        om = t if om is None else om + t
    y = y + _dot(om.astype(BF16), wout_ref[hw:, :])
    o_ref[...] = x_ref[...] + _rms(y, g_ref[3:4])


def _even_out(x, o_r, ret, o_lat, gains, gn, w_uv, w_out):
    n, d = x.shape
    tm = _row_tile(n)
    hw = H_A * DV_A
    return pl.pallas_call(
        _even_out_kernel,
        grid=(n // tm,),
        in_specs=[pl.BlockSpec((tm, d), lambda i: (i, 0)), pl.BlockSpec((tm, hw), lambda i: (i, 0)),
                  pl.BlockSpec((tm, hw), lambda i: (i, 3)),
                  pl.BlockSpec((H_B, tm, KV_RANK), lambda i: (0, i, 0)),
                  _resident(gains.shape), _resident(gn.shape), _resident(w_uv.shape), _resident(w_out.shape)],
        out_specs=pl.BlockSpec((tm, d), lambda i: (i, 0)),
        out_shape=jax.ShapeDtypeStruct((n, d), F32),
        compiler_params=_params("parallel"),
        name="even_out",
    )(x, o_r, ret, o_lat, gains, gn, w_uv, w_out)


def _odd_in_kernel(x_ref, g_ref, win_ref, bf_ref, qkv_ref, fk_ref, fv_ref, fl_ref):
    h = _rms(x_ref[...], g_ref[2:3]).astype(BF16)
    z = _dot(h, win_ref[...])
    w = H_C * HD_C
    qkv_ref[:, 0:w] = (z[:, 0:w] * FOX_SCALE).astype(BF16)
    qkv_ref[:, w:3 * w] = z[:, w:3 * w].astype(BF16)
    fk_ref[...] = z[:, w:2 * w]
    fv_ref[...] = z[:, 2 * w:3 * w]
    t = z[:, 3 * w:3 * w + LANES] + bf_ref[...]
    ls = jnp.minimum(t, 0.0) - jnp.log1p(jnp.exp(-jnp.abs(t)))
    fl_ref[...] = ls[:, :H_C]


def _odd_in(x, gains, w_in, b_f):
    n, d = x.shape
    tm = _row_tile(n)
    w = H_C * HD_C
    row = lambda c: pl.BlockSpec((tm, c), lambda i: (i, 0))
    return pl.pallas_call(
        _odd_in_kernel,
        grid=(n // tm,),
        in_specs=[row(d), _resident(gains.shape), _resident(w_in.shape), _resident(b_f.shape)],
        out_specs=[row(3 * w), row(w), row(w), row(H_C)],
        out_shape=[jax.ShapeDtypeStruct((n, 3 * w), BF16), jax.ShapeDtypeStruct((n, w), F32),
                   jax.ShapeDtypeStruct((n, w), F32), jax.ShapeDtypeStruct((n, H_C), F32)],
        compiler_params=_params("parallel"),
        name="odd_in",
    )(x, gains, w_in, b_f)


def _odd_out_kernel(x_ref, a_ref, g_ref, wout_ref, o_ref):
    o_ref[...] = x_ref[...] + _rms(_dot(a_ref[...], wout_ref[...]), g_ref[3:4])


def _odd_out(x, a, gains, w_out):
    n, d = x.shape
    tm = _row_tile(n)
    return pl.pallas_call(
        _odd_out_kernel,
        grid=(n // tm,),
        in_specs=[pl.BlockSpec((tm, d), lambda i: (i, 0)), pl.BlockSpec((tm, a.shape[1]), lambda i: (i, 0)),
                  _resident(gains.shape), _resident(w_out.shape)],
        out_specs=pl.BlockSpec((tm, d), lambda i: (i, 0)),
        out_shape=jax.ShapeDtypeStruct((n, d), F32),
        compiler_params=_params("parallel"),
        name="odd_out",
    )(x, a, gains, w_out)


def _ret_kernel(qkv_ref, s0_ref, dec_ref, qd_ref, kd_ref, sd_ref, o_ref, s_ref):
    @pl.when(pl.program_id(1) == 0)
    def _():
        s_ref[...] = s0_ref[...]

    for hh in range(H_A):
        q = qkv_ref[:, hh * DK_A:(hh + 1) * DK_A]
        k = qkv_ref[:, (H_A + hh) * DK_A:(H_A + hh + 1) * DK_A]
        v = qkv_ref[:, 2 * H_A * DK_A + hh * DV_A:2 * H_A * DK_A + (hh + 1) * DV_A].astype(BF16)
        s = s_ref[0, hh]
        qb = q.astype(BF16)
        scores = _dot_nt(qb, k.astype(BF16)) * dec_ref[hh]
        o = _dot(scores.astype(BF16), v) + _dot(qb, s.astype(BF16)) * qd_ref[hh]
        o_ref[:, hh * DV_A:(hh + 1) * DV_A] = o
        s_ref[0, hh] = sd_ref[hh] * s + _dot_tn((k * kd_ref[hh]).astype(BF16), v)


def _retention(qkv, s0, nb, nc, length):
    lp = RET_CHUNK
    log_gamma = jnp.log1p(-jnp.exp2(-5.0 - jnp.arange(H_A, dtype=F32)))
    idx = jnp.arange(lp, dtype=F32)
    diff = idx[:, None] - idx[None, :]
    decay = jnp.where(diff >= 0, jnp.exp(log_gamma[:, None, None] * jnp.maximum(diff, 0.0)), 0.0)
    ones = jnp.ones((1, 1, DV_A), F32)
    qd = jnp.exp(log_gamma[:, None] * (idx[None, :] + 1.0))[:, :, None] * ones
    kd = jnp.exp(log_gamma[:, None] * (length - 1.0 - idx[None, :]))[:, :, None] * ones
    sd = jnp.exp(log_gamma * length)[:, None, None] * ones
    wq = 3 * H_A * DK_A
    return pl.pallas_call(
        _ret_kernel,
        grid=(nb, nc),
        in_specs=[pl.BlockSpec((lp, wq), lambda b, c: (b * nc + c, 0)),
                  pl.BlockSpec((1, H_A, DK_A, DV_A), lambda b, c: (b, 0, 0, 0)),
                  _resident(decay.shape), _resident(qd.shape), _resident(kd.shape), _resident(sd.shape)],
        out_specs=[pl.BlockSpec((lp, H_A * DV_A), lambda b, c: (b * nc + c, 0)),
                   pl.BlockSpec((1, H_A, DK_A, DV_A), lambda b, c: (b, 0, 0, 0))],
        out_shape=[jax.ShapeDtypeStruct((nb * nc * lp, H_A * DV_A), F32),
                   jax.ShapeDtypeStruct((nb, H_A, DK_A, DV_A), F32)],
        compiler_params=_params("parallel", "arbitrary"),
        name="retention",
    )(qkv, s0, decay, qd, kd, sd)


def _online_update(m_ref, l_ref, acc_ref, s, pv):
    m_prev = m_ref[...]
    m_new = jnp.maximum(m_prev, jnp.max(s, axis=1, keepdims=True))
    alpha = jnp.exp(m_prev - m_new)
    p = jnp.exp(s - m_new)
    l_ref[...] = alpha * l_ref[...] + jnp.sum(p, axis=1, keepdims=True)
    acc_ref[...] = alpha * acc_ref[...] + pv(p.astype(BF16))
    m_ref[...] = m_new


def _init_state(m_ref, l_ref, acc_ref):
    m_ref[...] = jnp.full(m_ref.shape, NEG_INF, F32)
    l_ref[...] = jnp.zeros(l_ref.shape, F32)
    acc_ref[...] = jnp.zeros(acc_ref.shape, F32)


def _mla_flash_kernel(q_ref, k_ref, o_ref, m_ref, l_ref, acc_ref, *, tq, tk):
    i = pl.program_id(1)
    rows = H_B * tq
    q = q_ref[...].reshape(rows, q_ref.shape[2])
    _init_state(m_ref, l_ref, acc_ref)

    def scores(j):
        return _dot_nt(q, k_ref[pl.ds(pl.multiple_of(j * tk, tk), tk), :]) * MLA_SCALE

    def update(j, s, masked):
        off = pl.multiple_of(j * tk, tk)
        if masked:
            qpos = i * tq + (lax.broadcasted_iota(jnp.int32, (rows, tk), 0) & (tq - 1))
            kpos = off + lax.broadcasted_iota(jnp.int32, (rows, tk), 1)
            s = jnp.where(kpos <= qpos, s, NEG_INF)
        _online_update(m_ref, l_ref, acc_ref, s, lambda p: _dot(p, k_ref[pl.ds(off, tk), 0:KV_RANK]))

    nfull = (i * tq) // tk

    def body(j, s):
        nxt = scores(j + 1)
        update(j, s, False)
        return nxt

    update(nfull, lax.fori_loop(0, nfull, body, scores(0)), True)
    o = acc_ref[...] / l_ref[...]
    o_ref[...] = o.reshape(H_B, tq, KV_RANK).astype(BF16)


def _mla_prompt(qc, kc, nb, seq):
    tq = min(128, seq)
    tk = min(512, seq)
    nq = seq // tq
    kw = kc.shape[1]
    return pl.pallas_call(
        functools.partial(_mla_flash_kernel, tq=tq, tk=tk),
        grid=(nb, nq),
        in_specs=[pl.BlockSpec((H_B, tq, kw), lambda b, i: (0, b * nq + i, 0)),
                  pl.BlockSpec((seq, kw), lambda b, i: (b, 0))],
        out_specs=pl.BlockSpec((H_B, tq, KV_RANK), lambda b, i: (0, b * nq + i, 0)),
        out_shape=jax.ShapeDtypeStruct((H_B, nb * seq, KV_RANK), BF16),
        scratch_shapes=[pltpu.VMEM((H_B * tq, 1), F32), pltpu.VMEM((H_B * tq, 1), F32),
                        pltpu.VMEM((H_B * tq, KV_RANK), F32)],
        compiler_params=_params("parallel", "arbitrary"),
        name="mla_prompt",
    )(qc, kc)


def _cumsum_lanes(x, tri):
    hi = x.astype(BF16)
    r = x - hi.astype(F32)
    mid = r.astype(BF16)
    lo = (r - mid.astype(F32)).astype(BF16)
    cc = _dot(jnp.concatenate([hi, mid, lo], axis=0), tri)
    n = x.shape[0]
    return cc[0:n] + cc[n:2 * n] + cc[2 * n:3 * n]


def _cumsum_kernel(x_ref, tri_ref, o_ref, c_ref):
    @pl.when(pl.program_id(1) == 0)
    def _():
        c_ref[...] = jnp.zeros(c_ref.shape, F32)

    c = _cumsum_lanes(x_ref[0], tri_ref[...]) + c_ref[...]
    o_ref[0] = c
    c_ref[...] = jnp.broadcast_to(c[:, LANES - 1:LANES], c_ref.shape)


def _cumsum_seq(x_t, tri):
    nb, h, seq = x_t.shape
    return pl.pallas_call(
        _cumsum_kernel,
        grid=(nb, seq // LANES),
        in_specs=[pl.BlockSpec((1, h, LANES), lambda b, j: (b, 0, j)), _resident(tri.shape)],
        out_specs=pl.BlockSpec((1, h, LANES), lambda b, j: (b, 0, j)),
        out_shape=jax.ShapeDtypeStruct((nb, h, seq), F32),
        scratch_shapes=[pltpu.VMEM((h, LANES), F32)],
        compiler_params=_params("parallel", "arbitrary"),
        name="cumsum",
    )(x_t, tri)


def _fox_flash_kernel(q_ref, k_ref, v_ref, c_ref, o_ref, ma, la, acca, mb, lb, accb, *, tq, tk):
    i = pl.program_id(2)
    q2 = q_ref[...]
    lane = lax.broadcasted_iota(jnp.int32, q2.shape, 1)
    zero = jnp.zeros_like(q2)
    qa = jnp.where(lane < HD_C, q2, zero)
    qb = jnp.where(lane >= HD_C, q2, zero)
    _init_state(ma, la, acca)
    _init_state(mb, lb, accb)

    def scores(j):
        off = pl.multiple_of(j * tk, tk)
        k = k_ref[pl.ds(off, tk), :]
        cb = c_ref[0, :, pl.ds(off, tk)]
        return _dot_nt(qa, k) - cb[0:1, :], _dot_nt(qb, k) - cb[1:2, :]

    def update(j, ss, masked):
        off = pl.multiple_of(j * tk, tk)
        v = v_ref[pl.ds(off, tk), :]
        if masked:
            qpos = i * tq + lax.broadcasted_iota(jnp.int32, (tq, tk), 0)
            kpos = off + lax.broadcasted_iota(jnp.int32, (tq, tk), 1)
            keep = kpos <= qpos
        for s, st in zip(ss, ((ma, la, acca), (mb, lb, accb))):
            if masked:
                s = jnp.where(keep, s, NEG_INF)
            _online_update(*st, s, lambda p: _dot(p, v))

    nfull = (i * tq) // tk

    def body(j, ss):
        nxt = scores(j + 1)
        update(j, ss, False)
        return nxt

    ss = lax.fori_loop(0, nfull, body, scores(0))
    update(nfull, ss, True)
    lane_o = lax.broadcasted_iota(jnp.int32, (tq, LANES), 1)
    o = jnp.where(lane_o < HD_C, acca[...] / la[...], accb[...] / lb[...])
    o_ref[...] = o.astype(BF16)


def _fox_prompt(qkv, c_pairs, nb, seq):
    tq = min(256, seq)
    tk = min(512, seq)
    nq = seq // tq
    npair = H_C * HD_C // LANES
    st = lambda w: pltpu.VMEM((tq, w), F32)
    return pl.pallas_call(
        functools.partial(_fox_flash_kernel, tq=tq, tk=tk),
        grid=(nb, npair, nq),
        in_specs=[pl.BlockSpec((tq, LANES), lambda b, h, i: (b * nq + i, h)),
                  pl.BlockSpec((seq, LANES), lambda b, h, i: (b, npair + h)),
                  pl.BlockSpec((seq, LANES), lambda b, h, i: (b, 2 * npair + h)),
                  pl.BlockSpec((1, 2, seq), lambda b, h, i: (b * npair + h, 0, 0))],
        out_specs=pl.BlockSpec((tq, LANES), lambda b, h, i: (b * nq + i, h)),
        out_shape=jax.ShapeDtypeStruct((nb * seq, H_C * HD_C), BF16),
        scratch_shapes=[st(1), st(1), st(LANES), st(1), st(1), st(LANES)],
        compiler_params=_params("parallel", "parallel", "arbitrary"),
        name="fox_prompt",
    )(qkv, qkv, qkv, c_pairs)


def _mla_dec_kernel(pt_ref, q_ref, *refs, pp, t_new):
    del pt_ref
    ck_refs, kp_refs = refs[:pp], refs[pp:2 * pp]
    kn_ref, o_ref, m_ref, l_ref, acc_ref = refs[2 * pp:]
    c = pl.program_id(1)

    @pl.when(c == 0)
    def _():
        _init_state(m_ref, l_ref, acc_ref)

    q = q_ref[0]
    ql, qp = q[:, :KV_RANK], q[:, KV_RANK:KV_RANK + ROPE_B]
    cks = [r[0, 0].astype(BF16) for r in ck_refs]
    s = jnp.concatenate([_dot_nt(ql, ck) + _dot_nt(qp, kp[0, 0].astype(BF16))
                         for ck, kp in zip(cks, kp_refs)], axis=1) * MLA_SCALE

    def pv(p):
        out = None
        for n, ck in enumerate(cks):
            t = _dot(p[:, n * LANES:(n + 1) * LANES], ck)
            out = t if out is None else out + t
        return out

    _online_update(m_ref, l_ref, acc_ref, s, pv)

    @pl.when(c == pl.num_programs(1) - 1)
    def _():
        kn = kn_ref[0]
        sn = _dot_nt(q, kn) * MLA_SCALE
        tok = lax.broadcasted_iota(jnp.int32, sn.shape, 0) & (t_new - 1)
        col = lax.broadcasted_iota(jnp.int32, sn.shape, 1)
        sn = jnp.where(col <= tok, sn, NEG_INF)
        _online_update(m_ref, l_ref, acc_ref, sn, lambda p: _dot(p, kn[:, :KV_RANK]))
        o_ref[0] = acc_ref[...] / l_ref[...]


def _mla_decode(page_table, q, pool_ckv, pool_kpe, k_new, layer, t_new):
    bs, rows, kw = q.shape
    n_pages = page_table.shape[1]
    page = pool_ckv.shape[2]
    pp = min(16, n_pages)
    assert page == LANES and n_pages % pp == 0 and t_new & (t_new - 1) == 0
    nch = n_pages // pp
    pt = page_table.reshape(-1)

    def pool_spec(width, n):
        return pl.BlockSpec((1, 1, page, width),
                            lambda b, c, pt_ref: (layer, pt_ref[b * n_pages + c * pp + n], 0, 0))

    in_specs = ([pl.BlockSpec((1, rows, kw), lambda b, c, pt_ref: (b, 0, 0))]
                + [pool_spec(KV_RANK, n) for n in range(pp)] + [pool_spec(ROPE_B, n) for n in range(pp)]
                + [pl.BlockSpec((1, page, kw), lambda b, c, pt_ref: (b, 0, 0))])
    return pl.pallas_call(
        functools.partial(_mla_dec_kernel, pp=pp, t_new=t_new),
        grid_spec=pltpu.PrefetchScalarGridSpec(
            num_scalar_prefetch=1, grid=(bs, nch), in_specs=in_specs,
            out_specs=pl.BlockSpec((1, rows, KV_RANK), lambda b, c, pt_ref: (b, 0, 0)),
            scratch_shapes=[pltpu.VMEM((rows, 1), F32), pltpu.VMEM((rows, 1), F32),
                            pltpu.VMEM((rows, KV_RANK), F32)]),
        out_shape=jax.ShapeDtypeStruct((bs, rows, KV_RANK), F32),
        compiler_params=_params("parallel", "arbitrary"),
        name="mla_decode",
    )(pt, q, *([pool_ckv] * pp), *([pool_kpe] * pp), k_new)


def _fox_bias_kernel(pt_ref, *refs, pp):
    del pt_ref
    lf_refs = refs[:pp]
    lfn_ref, tri_ref, c_ref, cn_ref, carry_ref = refs[pp:]
    c = pl.program_id(1)

    @pl.when(c == 0)
    def _():
        carry_ref[...] = jnp.zeros(carry_ref.shape, F32)

    tri = tri_ref[...]
    for n, r in enumerate(lf_refs):
        cpage = _cumsum_lanes(r[0, 0], tri) + carry_ref[...]
        carry_ref[...] = jnp.broadcast_to(cpage[:, LANES - 1:LANES], carry_ref.shape)
        c_ref[0, :, n * LANES:(n + 1) * LANES] = cpage

    @pl.when(c == pl.num_programs(1) - 1)
    def _():
        cn_ref[0] = _cumsum_lanes(lfn_ref[0], tri) + carry_ref[...]


def _fox_bias(page_table, pool_lf_t, lf_new_t, tri, layer):
    bs, n_pages = page_table.shape
    page = pool_lf_t.shape[3]
    pp = min(32, n_pages)
    assert page == LANES and n_pages % pp == 0
    pt = page_table.reshape(-1)

    def pool_spec(n):
        return pl.BlockSpec((1, 1, H_C, page), lambda b, c, pt_ref: (layer, pt_ref[b * n_pages + c * pp + n], 0, 0))

    in_specs = ([pool_spec(n) for n in range(pp)]
                + [pl.BlockSpec((1, H_C, page), lambda b, c, pt_ref: (b, 0, 0)),
                   pl.BlockSpec(tri.shape, lambda b, c, pt_ref: (0, 0))])
    return pl.pallas_call(
        functools.partial(_fox_bias_kernel, pp=pp),
        grid_spec=pltpu.PrefetchScalarGridSpec(
            num_scalar_prefetch=1, grid=(bs, n_pages // pp), in_specs=in_specs,
            out_specs=[pl.BlockSpec((1, H_C, pp * page), lambda b, c, pt_ref: (b, 0, c)),
                       pl.BlockSpec((1, H_C, page), lambda b, c, pt_ref: (b, 0, 0))],
            scratch_shapes=[pltpu.VMEM((H_C, LANES), F32)]),
        out_shape=[jax.ShapeDtypeStruct((bs, H_C, n_pages * page), F32),
                   jax.ShapeDtypeStruct((bs, H_C, page), F32)],
        compiler_params=_params("parallel", "arbitrary"),
        name="fox_bias",
    )(pt, *([pool_lf_t] * pp), lf_new_t, tri)


def _fox_dec_kernel(pt_ref, q_ref, *refs, pp, group):
    del pt_ref
    k_refs, v_refs = refs[:pp], refs[pp:2 * pp]
    b_ref, kn_ref, vn_ref, bn_ref, o_ref, m_ref, l_ref, acc_ref = refs[2 * pp:]
    c = pl.program_id(1)

    @pl.when(c == 0)
    def _():
        _init_state(m_ref, l_ref, acc_ref)

    q = q_ref[0]
    rows = q.shape[0]
    flat = k_refs[0].shape[2] * H_C
    nrow = flat // LANES

    def same_head(width):
        row_h = lax.broadcasted_iota(jnp.int32, (rows, width), 0) % H_C
        lane_h = lax.broadcasted_iota(jnp.int32, (rows, width), 1) % H_C
        return row_h == lane_h

    def pv_of(vs, width):
        def pv(p):
            out = None
            for n, v in enumerate(vs):
                t = _dot(p[:, n * width:(n + 1) * width], v)
                out = t if out is None else out + t
            return out
        return pv

    keep = same_head(group * flat)
    for g0 in range(0, pp, group):
        ks = [k_refs[n][0, 0].reshape(flat, HD_C).astype(BF16) for n in range(g0, g0 + group)]
        vs = [v_refs[n][0, 0].reshape(flat, HD_C).astype(BF16) for n in range(g0, g0 + group)]
        s = jnp.concatenate([_dot_nt(q, kf) for kf in ks], axis=1)
        bias = jnp.concatenate([b_ref[0, n, r:r + 1, :] for n in range(g0, g0 + group) for r in range(nrow)], axis=1)
        s = jnp.where(keep, s - bias, NEG_INF)
        _online_update(m_ref, l_ref, acc_ref, s, pv_of(vs, flat))

    @pl.when(c == pl.num_programs(1) - 1)
    def _():
        sn = _dot_nt(q, kn_ref[0]) - bn_ref[0]
        tok = lax.broadcasted_iota(jnp.int32, sn.shape, 0) // H_C
        new = lax.broadcasted_iota(jnp.int32, sn.shape, 1) // H_C
        sn = jnp.where(same_head(sn.shape[1]) & (new <= tok), sn, NEG_INF)
        _online_update(m_ref, l_ref, acc_ref, sn, pv_of([vn_ref[0]], sn.shape[1]))
        o_ref[0] = acc_ref[...] / l_ref[...]


def _fox_decode(page_table, q, pool_k, pool_v, bias, k_new, v_new, bias_new, layer):
    bs, rows, hd = q.shape
    n_pages = page_table.shape[1]
    page = pool_k.shape[2]
    pp = min(8, n_pages)
    group = min(2, pp)
    assert n_pages % pp == 0 and pp % group == 0 and (page * H_C) % LANES == 0
    pt = page_table.reshape(-1)

    def pool_spec(n):
        return pl.BlockSpec((1, 1, page, H_C, hd),
                            lambda b, c, pt_ref: (layer, pt_ref[b * n_pages + c * pp + n], 0, 0, 0))

    per_seq = lambda shape: pl.BlockSpec((1,) + shape, lambda b, c, pt_ref: (b, 0, 0))
    in_specs = ([per_seq((rows, hd))] + [pool_spec(n) for n in range(pp)] + [pool_spec(n) for n in range(pp)]
                + [pl.BlockSpec((1, pp) + bias.shape[2:], lambda b, c, pt_ref: (b, c, 0, 0)),
                   per_seq(k_new.shape[1:]), per_seq(v_new.shape[1:]), per_seq(bias_new.shape[1:])])
    return pl.pallas_call(
        functools.partial(_fox_dec_kernel, pp=pp, group=group),
        grid_spec=pltpu.PrefetchScalarGridSpec(
            num_scalar_prefetch=1, grid=(bs, n_pages // pp), in_specs=in_specs,
            out_specs=per_seq((rows, hd)),
            scratch_shapes=[pltpu.VMEM((rows, 1), F32), pltpu.VMEM((rows, 1), F32), pltpu.VMEM((rows, hd), F32)]),
        out_shape=jax.ShapeDtypeStruct((bs, rows, hd), F32),
        compiler_params=_params("parallel", "arbitrary"),
        name="fox_decode",
    )(pt, q, *([pool_k] * pp), *([pool_v] * pp), bias, k_new, v_new, bias_new)


def _rope_tables(pos):
    pos = pos.astype(F32)[:, None]

    def cs(half):
        inv = ROPE_THETA ** (-jnp.arange(half, dtype=F32) / half)
        ang = pos * inv[None, :]
        return jnp.cos(ang), jnp.sin(ang)

    c, s = cs(DK_A // 2)
    rope_a = jnp.concatenate([c, c, -s, s], axis=1)
    c, s = cs(ROPE_B // 2)
    z = lambda w: jnp.zeros((pos.shape[0], w), F32)
    rope_b = jnp.concatenate([c, c, z(LANES - ROPE_B), -s, z(LANES - ROPE_B // 2),
                              z(ROPE_B // 2), s, z(LANES - ROPE_B)], axis=1)
    return rope_a, rope_b


def _pad_cols(w, width):
    return jnp.pad(w, ((0, 0), (0, width - w.shape[1])))


def kernel(x_prompt, x_sample, state_ret, cache_mla_ckv, cache_mla_kpe, cache_fox_k, cache_fox_v,
           cache_fox_logf, page_table, p_prompt, p_sample, norm_gains, ffn_w_in, ffn_w_out, ple_w_gate,
           ple_w_proj, w_in0, ret_gn, mla_q_norm, mla_w_uq, mla_kv_norm, mla_w_uk, mla_w_uv, w_out0,
           w_in1, fox_b_f, w_out1):
    nb, seq, d = x_prompt.shape
    bs, t_new, _ = x_sample.shape
    depth = norm_gains.shape[0]
    n_p, n_s = nb * seq, bs * t_new
    n_pages, page = page_table.shape[1], cache_mla_ckv.shape[2]
    past_len = n_pages * page
    assert seq % RET_CHUNK == 0 and t_new <= RET_CHUNK

    x = jnp.concatenate([x_prompt.reshape(n_p, d), x_sample.reshape(n_s, d)], axis=0)
    pos = jnp.concatenate([jnp.tile(jnp.arange(seq, dtype=jnp.int32), nb),
                           jnp.tile(past_len + jnp.arange(t_new, dtype=jnp.int32), bs)])
    rope_a, rope_b = _rope_tables(pos)
    eye_b = jnp.eye(H_B, dtype=F32)
    ii = jnp.arange(LANES)
    tri = (ii[:, None] <= ii[None, :]).astype(BF16)

    rets_p, rets_s, ckvs, kpes, fks, fvs, fls = [], [], [], [], [], [], []
    for i in range(depth):
        g = norm_gains[i]
        j = i // 2
        x = _ffn(x, g, ffn_w_in[i, 0].astype(BF16), ffn_w_out[i, 0].astype(BF16), 0)
        if i % 2 == 0:
            w_in = _pad_cols(w_in0[j], IN0_PAD).astype(BF16)
            uq = mla_w_uq[j]
            w_uq = jnp.concatenate(
                [uq[:, :, :NOPE_B].reshape(Q_RANK, H_B * NOPE_B),
                 jnp.pad(uq[:, :, NOPE_B:], ((0, 0), (0, 0), (0, LANES - ROPE_B))).reshape(Q_RANK, H_B * LANES)],
                axis=1).astype(BF16)
            w_bd = jnp.einsum('hdr,hg->hdgr', jnp.transpose(mla_w_uk[j], (1, 2, 0)), eye_b)
            w_bd = w_bd.reshape(H_B * NOPE_B, H_B * KV_RANK).astype(BF16)
            w_uv = jnp.einsum('rhe,hg->hrge', mla_w_uv[j], eye_b).reshape(H_B, KV_RANK, H_B * VD_B).astype(BF16)
            ret, qc, kc, ckv, kpe = _even_in(x, g, w_in, rope_a, rope_b, mla_q_norm[j][None], mla_kv_norm[j][None],
                                             w_uq, w_bd)
            wq = 3 * H_A * DK_A
            o_rp, s_p = _retention(ret, jnp.zeros((nb, H_A, DK_A, DV_A), F32), nb, seq // RET_CHUNK, float(RET_CHUNK))
            ret_s = jnp.pad(ret[n_p:, :wq].reshape(bs, t_new, wq), ((0, 0), (0, RET_CHUNK - t_new), (0, 0)))
            o_rs, s_s = _retention(ret_s.reshape(bs * RET_CHUNK, wq), state_ret[j], bs, 1, float(t_new))
            o_rs = o_rs.reshape(bs, RET_CHUNK, H_A * DV_A)[:, :t_new].reshape(n_s, H_A * DV_A)
            o_r = jnp.concatenate([o_rp, o_rs], axis=0)
            ol_p = _mla_prompt(qc, kc, nb, seq)
            kw = kc.shape[1]
            q_s = qc[:, n_p:].reshape(H_B, bs, t_new, kw).transpose(1, 0, 2, 3).reshape(bs, H_B * t_new, kw)
            k_new = jnp.pad(kc[n_p:].reshape(bs, t_new, kw), ((0, 0), (0, page - t_new), (0, 0)))
            ol_s = _mla_decode(page_table, q_s, cache_mla_ckv, cache_mla_kpe, k_new, j, t_new)
            ol_s = ol_s.reshape(bs, H_B, t_new, KV_RANK).transpose(1, 0, 2, 3).reshape(H_B, n_s, KV_RANK)
            o_lat = jnp.concatenate([ol_p, ol_s.astype(BF16)], axis=1)
            x = _even_out(x, o_r, ret, o_lat, g, ret_gn[j].reshape(1, H_A * DV_A), w_uv, w_out0[j].astype(BF16))
            rets_p.append(s_p)
            rets_s.append(s_s)
            ckvs.append(ckv)
            kpes.append(kpe)
        else:
            w = H_C * HD_C
            w_in = _pad_cols(w_in1[j], IN1_PAD).astype(BF16)
            b_f = jnp.pad(fox_b_f[j], (0, LANES - H_C))[None]
            qkv, fk, fv, fl = _odd_in(x, g, w_in, b_f)
            c_t = _cumsum_seq(fl[:n_p].reshape(nb, seq, H_C).transpose(0, 2, 1), tri)
            o_p = _fox_prompt(qkv, c_t.reshape(nb * H_C // 2, 2, seq), nb, seq)
            rows_s = t_new * H_C
            assert rows_s <= LANES
            lf_new_t = jnp.pad(fl[n_p:].reshape(bs, t_new, H_C).transpose(0, 2, 1),
                               ((0, 0), (0, 0), (0, page - t_new)))
            c_past, c_new = _fox_bias(page_table, cache_fox_logf.transpose(0, 1, 3, 2), lf_new_t, tri, j)
            bias = c_past.transpose(0, 2, 1).reshape(bs, n_pages, page * H_C // LANES, LANES)
            bias_new = c_new[:, :, :LANES // H_C].transpose(0, 2, 1).reshape(bs, 1, LANES)
            flat_rows = lambda a: jnp.pad(a.reshape(bs, rows_s, HD_C), ((0, 0), (0, LANES - rows_s), (0, 0)))
            o_s = _fox_decode(page_table, qkv[n_p:, :w].reshape(bs, rows_s, HD_C), cache_fox_k, cache_fox_v, bias,
                              flat_rows(qkv[n_p:, w:2 * w]), flat_rows(qkv[n_p:, 2 * w:]), bias_new, j)
            a = jnp.concatenate([o_p, o_s.reshape(n_s, w).astype(BF16)], axis=0)
            x = _odd_out(x, a, g, w_out1[j].astype(BF16))
            fks.append(fk)
            fvs.append(fv)
            fls.append(fl)
        x = _ffn(x, g, ffn_w_in[i, 1].astype(BF16), ffn_w_out[i, 1].astype(BF16), 4)
        p = jnp.concatenate([p_prompt[i].reshape(n_p, -1), p_sample[i].reshape(n_s, -1)], axis=0)
        x = _ple(x, p, g, ple_w_gate[i].astype(BF16), ple_w_proj[i].astype(BF16))

    def split(parts, tail):
        a = jnp.stack(parts)
        return (a[:, :n_p].reshape((len(parts), nb, seq) + tail),
                a[:, n_p:].reshape((len(parts), bs, t_new) + tail))

    ckv_p, ckv_s = split(ckvs, (KV_RANK,))
    kpe_p, kpe_s = split(kpes, (ROPE_B,))
    fk_p, fk_s = split(fks, (H_C, HD_C))
    fv_p, fv_s = split(fvs, (H_C, HD_C))
    fl_p, fl_s = split(fls, (H_C,))
    return (x[:n_p].reshape(nb, seq, d), x[n_p:].reshape(bs, t_new, d), jnp.stack(rets_p), jnp.stack(rets_s),
            ckv_p, ckv_s, kpe_p, kpe_s, fk_p, fk_s, fv_p, fv_s, fl_p, fl_s)
```

```python
import functools

import jax
import jax.numpy as jnp
import numpy as np
from jax import lax
from jax.experimental import pallas as pl
from jax.experimental.pallas import tpu as pltpu

F32 = jnp.float32
BF16 = jnp.bfloat16

H_A, DK_A, DV_A = 4, 128, 128
RET_CHUNK = 128
H_B, NOPE_B, ROPE_B, VD_B = 8, 64, 32, 64
Q_RANK, KV_RANK = 256, 128
H_C, HD_C = 16, 64
ROPE_THETA = 10000.0
NORM_EPS = 1e-6
NEG_INF = -1e30
MLA_SCALE = (NOPE_B + ROPE_B) ** -0.5
FOX_SCALE = HD_C ** -0.5

LANES = 128
IN0_PAD = 2560
IN1_PAD = 3200
VMEM_LIMIT = 56 * 1024 * 1024

_NT = (((1,), (1,)), ((), ()))
_TN = (((0,), (0,)), ((), ()))


def _dot(a, b):
    return jnp.dot(a, b, preferred_element_type=F32)


def _dot_nt(a, b):
    return lax.dot_general(a, b, _NT, preferred_element_type=F32)


def _dot_tn(a, b):
    return lax.dot_general(a, b, _TN, preferred_element_type=F32)


def _rms(x, g):
    return x * lax.rsqrt(jnp.mean(x * x, axis=-1, keepdims=True) + NORM_EPS) * g


def _row_tile(n, cap=512):
    t = cap
    while n % t:
        t //= 2
    assert t >= 8, n
    return t


def _params(*sem):
    return pltpu.CompilerParams(dimension_semantics=sem, vmem_limit_bytes=VMEM_LIMIT)


def _resident(shape):
    nd = len(shape)
    return pl.BlockSpec(shape, lambda *_: (0,) * nd, pipeline_mode=pl.Buffered(1))


def _ffn_body(x, g_pre, g_post, win_ref, wout_ref, dff, chunk):
    h = _rms(x, g_pre).astype(BF16)
    acc = jnp.zeros(x.shape, F32)
    for c0 in range(0, dff, chunk):
        a = _dot(h, win_ref[:, c0:c0 + chunk])
        b = _dot(h, win_ref[:, dff + c0:dff + c0 + chunk])
        act = (a * jax.nn.sigmoid(a)) * b
        acc = acc + _dot(act.astype(BF16), wout_ref[c0:c0 + chunk, :])
    return x + 0.5 * _rms(acc, g_post)


def _ffn_kernel(x_ref, g_ref, win_ref, wout_ref, o_ref, *, dff, chunk, gi):
    o_ref[...] = _ffn_body(x_ref[...], g_ref[gi:gi + 1], g_ref[gi + 1:gi + 2], win_ref, wout_ref, dff, chunk)


def _ffn(x, gains, w_in, w_out, gi):
    n, d = x.shape
    dff = w_out.shape[0]
    chunk = 256 if dff % 256 == 0 else LANES
    tm = _row_tile(n)
    return pl.pallas_call(
        functools.partial(_ffn_kernel, dff=dff, chunk=chunk, gi=gi),
        grid=(n // tm,),
        in_specs=[pl.BlockSpec((tm, d), lambda i: (i, 0)), _resident(gains.shape),
                  _resident(w_in.shape), _resident(w_out.shape)],
        out_specs=pl.BlockSpec((tm, d), lambda i: (i, 0)),
        out_shape=jax.ShapeDtypeStruct((n, d), F32),
        compiler_params=_params("parallel"),
        name="ffn",
    )(x, gains, w_in, w_out)


def _ple_kernel(x_ref, p_ref, g_ref, wg_ref, wp_ref, o_ref):
    x = x_ref[...]
    gate = jax.nn.sigmoid(_dot(_rms(x, g_ref[6:7]).astype(BF16), wg_ref[...]))
    e = _dot(p_ref[...].astype(BF16), wp_ref[...])
    o_ref[...] = x + _rms(gate * e, g_ref[7:8])


def _ple(x, p, gains, w_gate, w_proj):
    n, d = x.shape
    tm = _row_tile(n)
    return pl.pallas_call(
        _ple_kernel,
        grid=(n // tm,),
        in_specs=[pl.BlockSpec((tm, d), lambda i: (i, 0)), pl.BlockSpec((tm, p.shape[1]), lambda i: (i, 0)),
                  _resident(gains.shape), _resident(w_gate.shape), _resident(w_proj.shape)],
        out_specs=pl.BlockSpec((tm, d), lambda i: (i, 0)),
        out_shape=jax.ShapeDtypeStruct((n, d), F32),
        compiler_params=_params("parallel"),
        name="ple",
    )(x, p, gains, w_gate, w_proj)


def _even_in_kernel(x_ref, g_ref, win_ref, ra_ref, rb_ref, qn_ref, kvn_ref, wuq_ref, wbd_ref,
                    ret_ref, qc_ref, kc_ref, ckv_ref, kpe_ref):
    h = _rms(x_ref[...], g_ref[2:3]).astype(BF16)
    z = _dot(h, win_ref[...])
    hw = H_A * DK_A
    cosr, sinr = ra_ref[:, 0:LANES], ra_ref[:, LANES:2 * LANES]
    for hh in range(H_A):
        for base, sc in ((0, None), (hw, DK_A ** -0.5)):
            xh = z[:, base + hh * DK_A:base + (hh + 1) * DK_A]
            r = xh * cosr + pltpu.roll(xh, DK_A // 2, 1) * sinr
            ret_ref[:, base + hh * DK_A:base + (hh + 1) * DK_A] = r if sc is None else r * sc
    ret_ref[:, 2 * hw:4 * hw] = z[:, 2 * hw:4 * hw]

    c_cos, s_lo, s_hi = rb_ref[:, 0:LANES], rb_ref[:, LANES:2 * LANES], rb_ref[:, 2 * LANES:3 * LANES]

    def rope32(v):
        return (v * c_cos + pltpu.roll(v, LANES - ROPE_B // 2, 1) * s_lo
                + pltpu.roll(v, ROPE_B // 2, 1) * s_hi)

    o = 4 * hw
    cq = _rms(z[:, o:o + Q_RANK], qn_ref[...]).astype(BF16)
    q = _dot(cq, wuq_ref[...])
    nq = H_B * NOPE_B
    qlat = _dot(q[:, :nq].astype(BF16), wbd_ref[...])
    ckv = _rms(z[:, o + Q_RANK:o + Q_RANK + KV_RANK], kvn_ref[...])
    kpe = rope32(z[:, o + Q_RANK + KV_RANK:o + Q_RANK + KV_RANK + LANES])
    ckv_ref[...] = ckv
    kpe_ref[...] = kpe[:, :ROPE_B]
    kc_ref[:, 0:KV_RANK] = ckv.astype(BF16)
    kc_ref[:, KV_RANK:KV_RANK + LANES] = kpe.astype(BF16)
    for hh in range(H_B):
        qc_ref[hh, :, 0:KV_RANK] = qlat[:, hh * KV_RANK:(hh + 1) * KV_RANK].astype(BF16)
        qc_ref[hh, :, KV_RANK:KV_RANK + LANES] = rope32(q[:, nq + hh * LANES:nq + (hh + 1) * LANES]).astype(BF16)


def _even_in(x, gains, w_in, rope_a, rope_b, qn, kvn, w_uq, w_bd):
    n, d = x.shape
    tm = _row_tile(n)
    kw = KV_RANK + LANES
    row = lambda w: pl.BlockSpec((tm, w), lambda i: (i, 0))
    return pl.pallas_call(
        _even_in_kernel,
        grid=(n // tm,),
        in_specs=[row(d), _resident(gains.shape), _resident(w_in.shape), row(rope_a.shape[1]),
                  row(rope_b.shape[1]), _resident(qn.shape), _resident(kvn.shape), _resident(w_uq.shape),
                  _resident(w_bd.shape)],
        out_specs=[row(4 * H_A * DK_A), pl.BlockSpec((H_B, tm, kw), lambda i: (0, i, 0)), row(kw),
                   row(KV_RANK), row(ROPE_B)],
        out_shape=[jax.ShapeDtypeStruct((n, 4 * H_A * DK_A), F32), jax.ShapeDtypeStruct((H_B, n, kw), BF16),
                   jax.ShapeDtypeStruct((n, kw), BF16), jax.ShapeDtypeStruct((n, KV_RANK), F32),
                   jax.ShapeDtypeStruct((n, ROPE_B), F32)],
        compiler_params=_params("parallel"),
        name="even_in",
    )(x, gains, w_in, rope_a, rope_b, qn, kvn, w_uq, w_bd)


def _even_out_kernel(x_ref, or_ref, gr_ref, ol_ref, g_ref, gn_ref, wuv_ref, wout_ref, o_ref):
    hw = H_A * DV_A
    gate = gr_ref[...]
    gate = gate * jax.nn.sigmoid(gate)
    y = None
    for hh in range(H_A):
        sl = slice(hh * DV_A, (hh + 1) * DV_A)
        oh = or_ref[:, sl]
        xc = oh - jnp.mean(oh, axis=-1, keepdims=True)
        var = jnp.mean(xc * xc, axis=-1, keepdims=True)
        a = gate[:, sl] * (xc * lax.rsqrt(var + NORM_EPS) * gn_ref[:, sl])
        t = _dot(a.astype(BF16), wout_ref[sl, :])
        y = t if y is None else y + t
    om = None
    for hh in range(H_B):
        t = _dot(ol_ref[hh], wuv_ref[hh])
        om = t if om is None else om + t
    y = y + _dot(om.astype(BF16), wout_ref[hw:, :])
    o_ref[...] = x_ref[...] + _rms(y, g_ref[3:4])


def _even_out(x, o_r, ret, o_lat, gains, gn, w_uv, w_out):
    n, d = x.shape
    tm = _row_tile(n)
    hw = H_A * DV_A
    return pl.pallas_call(
        _even_out_kernel,
        grid=(n // tm,),
        in_specs=[pl.BlockSpec((tm, d), lambda i: (i, 0)), pl.BlockSpec((tm, hw), lambda i: (i, 0)),
                  pl.BlockSpec((tm, hw), lambda i: (i, 3)),
                  pl.BlockSpec((H_B, tm, KV_RANK), lambda i: (0, i, 0)),
                  _resident(gains.shape), _resident(gn.shape), _resident(w_uv.shape), _resident(w_out.shape)],
        out_specs=pl.BlockSpec((tm, d), lambda i: (i, 0)),
        out_shape=jax.ShapeDtypeStruct((n, d), F32),
        compiler_params=_params("parallel"),
        name="even_out",
    )(x, o_r, ret, o_lat, gains, gn, w_uv, w_out)


def _odd_in_kernel(x_ref, g_ref, win_ref, bf_ref, qkv_ref, fk_ref, fv_ref, fl_ref):
    h = _rms(x_ref[...], g_ref[2:3]).astype(BF16)
    z = _dot(h, win_ref[...])
    w = H_C * HD_C
    qkv_ref[:, 0:w] = (z[:, 0:w] * FOX_SCALE).astype(BF16)
    qkv_ref[:, w:3 * w] = z[:, w:3 * w].astype(BF16)
    fk_ref[...] = z[:, w:2 * w]
    fv_ref[...] = z[:, 2 * w:3 * w]
    t = z[:, 3 * w:3 * w + LANES] + bf_ref[...]
    ls = jnp.minimum(t, 0.0) - jnp.log1p(jnp.exp(-jnp.abs(t)))
    fl_ref[...] = ls[:, :H_C]


def _odd_in(x, gains, w_in, b_f):
    n, d = x.shape
    tm = _row_tile(n)
    w = H_C * HD_C
    row = lambda c: pl.BlockSpec((tm, c), lambda i: (i, 0))
    return pl.pallas_call(
        _odd_in_kernel,
        grid=(n // tm,),
        in_specs=[row(d), _resident(gains.shape), _resident(w_in.shape), _resident(b_f.shape)],
        out_specs=[row(3 * w), row(w), row(w), row(H_C)],
        out_shape=[jax.ShapeDtypeStruct((n, 3 * w), BF16), jax.ShapeDtypeStruct((n, w), F32),
                   jax.ShapeDtypeStruct((n, w), F32), jax.ShapeDtypeStruct((n, H_C), F32)],
        compiler_params=_params("parallel"),
        name="odd_in",
    )(x, gains, w_in, b_f)


def _odd_out_kernel(x_ref, a_ref, g_ref, wout_ref, o_ref):
    o_ref[...] = x_ref[...] + _rms(_dot(a_ref[...], wout_ref[...]), g_ref[3:4])


def _odd_out(x, a, gains, w_out):
    n, d = x.shape
    tm = _row_tile(n)
    return pl.pallas_call(
        _odd_out_kernel,
        grid=(n // tm,),
        in_specs=[pl.BlockSpec((tm, d), lambda i: (i, 0)), pl.BlockSpec((tm, a.shape[1]), lambda i: (i, 0)),
                  _resident(gains.shape), _resident(w_out.shape)],
        out_specs=pl.BlockSpec((tm, d), lambda i: (i, 0)),
        out_shape=jax.ShapeDtypeStruct((n, d), F32),
        compiler_params=_params("parallel"),
        name="odd_out",
    )(x, a, gains, w_out)


def _ret_kernel(qkv_ref, s0_ref, dec_ref, qd_ref, kd_ref, sd_ref, o_ref, s_ref):
    @pl.when(pl.program_id(1) == 0)
    def _():
        s_ref[...] = s0_ref[...]

    for hh in range(H_A):
        q = qkv_ref[:, hh * DK_A:(hh + 1) * DK_A]
        k = qkv_ref[:, (H_A + hh) * DK_A:(H_A + hh + 1) * DK_A]
        v = qkv_ref[:, 2 * H_A * DK_A + hh * DV_A:2 * H_A * DK_A + (hh + 1) * DV_A].astype(BF16)
        s = s_ref[0, hh]
        qb = q.astype(BF16)
        scores = _dot_nt(qb, k.astype(BF16)) * dec_ref[hh]
        o = _dot(scores.astype(BF16), v) + _dot(qb, s.astype(BF16)) * qd_ref[hh]
        o_ref[:, hh * DV_A:(hh + 1) * DV_A] = o
        s_ref[0, hh] = sd_ref[hh] * s + _dot_tn((k * kd_ref[hh]).astype(BF16), v)


def _retention(qkv, s0, nb, nc, length):
    lp = RET_CHUNK
    log_gamma = jnp.log1p(-jnp.exp2(-5.0 - jnp.arange(H_A, dtype=F32)))
    idx = jnp.arange(lp, dtype=F32)
    diff = idx[:, None] - idx[None, :]
    decay = jnp.where(diff >= 0, jnp.exp(log_gamma[:, None, None] * jnp.maximum(diff, 0.0)), 0.0)
    ones = jnp.ones((1, 1, DV_A), F32)
    qd = jnp.exp(log_gamma[:, None] * (idx[None, :] + 1.0))[:, :, None] * ones
    kd = jnp.exp(log_gamma[:, None] * (length - 1.0 - idx[None, :]))[:, :, None] * ones
    sd = jnp.exp(log_gamma * length)[:, None, None] * ones
    wq = 3 * H_A * DK_A
    return pl.pallas_call(
        _ret_kernel,
        grid=(nb, nc),
        in_specs=[pl.BlockSpec((lp, wq), lambda b, c: (b * nc + c, 0)),
                  pl.BlockSpec((1, H_A, DK_A, DV_A), lambda b, c: (b, 0, 0, 0)),
                  _resident(decay.shape), _resident(qd.shape), _resident(kd.shape), _resident(sd.shape)],
        out_specs=[pl.BlockSpec((lp, H_A * DV_A), lambda b, c: (b * nc + c, 0)),
                   pl.BlockSpec((1, H_A, DK_A, DV_A), lambda b, c: (b, 0, 0, 0))],
        out_shape=[jax.ShapeDtypeStruct((nb * nc * lp, H_A * DV_A), F32),
                   jax.ShapeDtypeStruct((nb, H_A, DK_A, DV_A), F32)],
        compiler_params=_params("parallel", "arbitrary"),
        name="retention",
    )(qkv, s0, decay, qd, kd, sd)


def _online_update(m_ref, l_ref, acc_ref, s, pv):
    m_prev = m_ref[...]
    m_new = jnp.maximum(m_prev, jnp.max(s, axis=1, keepdims=True))
    alpha = jnp.exp(m_prev - m_new)
    p = jnp.exp(s - m_new)
    l_ref[...] = alpha * l_ref[...] + jnp.sum(p, axis=1, keepdims=True)
    acc_ref[...] = alpha * acc_ref[...] + pv(p.astype(BF16))
    m_ref[...] = m_new


def _init_state(m_ref, l_ref, acc_ref):
    m_ref[...] = jnp.full(m_ref.shape, NEG_INF, F32)
    l_ref[...] = jnp.zeros(l_ref.shape, F32)
    acc_ref[...] = jnp.zeros(acc_ref.shape, F32)


def _mla_flash_kernel(q_ref, k_ref, o_ref, m_ref, l_ref, acc_ref, *, tq, tk):
    i = pl.program_id(1)
    rows = H_B * tq
    q = q_ref[...].reshape(rows, q_ref.shape[2])
    _init_state(m_ref, l_ref, acc_ref)

    def scores(j):
        return _dot_nt(q, k_ref[pl.ds(pl.multiple_of(j * tk, tk), tk), :]) * MLA_SCALE

    def update(j, s, masked):
        off = pl.multiple_of(j * tk, tk)
        if masked:
            qpos = i * tq + (lax.broadcasted_iota(jnp.int32, (rows, tk), 0) & (tq - 1))
            kpos = off + lax.broadcasted_iota(jnp.int32, (rows, tk), 1)
            s = jnp.where(kpos <= qpos, s, NEG_INF)
        _online_update(m_ref, l_ref, acc_ref, s, lambda p: _dot(p, k_ref[pl.ds(off, tk), 0:KV_RANK]))

    nfull = (i * tq) // tk

    def body(j, s):
        nxt = scores(j + 1)
        update(j, s, False)
        return nxt

    update(nfull, lax.fori_loop(0, nfull, body, scores(0)), True)
    o = acc_ref[...] / l_ref[...]
    o_ref[...] = o.reshape(H_B, tq, KV_RANK).astype(BF16)


def _mla_prompt(qc, kc, nb, seq):
    tq = min(128, seq)
    tk = min(512, seq)
    nq = seq // tq
    kw = kc.shape[1]
    return pl.pallas_call(
        functools.partial(_mla_flash_kernel, tq=tq, tk=tk),
        grid=(nb, nq),
        in_specs=[pl.BlockSpec((H_B, tq, kw), lambda b, i: (0, b * nq + i, 0)),
                  pl.BlockSpec((seq, kw), lambda b, i: (b, 0))],
        out_specs=pl.BlockSpec((H_B, tq, KV_RANK), lambda b, i: (0, b * nq + i, 0)),
        out_shape=jax.ShapeDtypeStruct((H_B, nb * seq, KV_RANK), BF16),
        scratch_shapes=[pltpu.VMEM((H_B * tq, 1), F32), pltpu.VMEM((H_B * tq, 1), F32),
                        pltpu.VMEM((H_B * tq, KV_RANK), F32)],
        compiler_params=_params("parallel", "arbitrary"),
        name="mla_prompt",
    )(qc, kc)


def _cumsum_lanes(x, tri):
    hi = x.astype(BF16)
    r = x - hi.astype(F32)
    mid = r.astype(BF16)
    lo = (r - mid.astype(F32)).astype(BF16)
    cc = _dot(jnp.concatenate([hi, mid, lo], axis=0), tri)
    n = x.shape[0]
    return cc[0:n] + cc[n:2 * n] + cc[2 * n:3 * n]


def _cumsum_kernel(x_ref, tri_ref, o_ref, c_ref):
    @pl.when(pl.program_id(1) == 0)
    def _():
        c_ref[...] = jnp.zeros(c_ref.shape, F32)

    c = _cumsum_lanes(x_ref[0], tri_ref[...]) + c_ref[...]
    o_ref[0] = c
    c_ref[...] = jnp.broadcast_to(c[:, LANES - 1:LANES], c_ref.shape)


def _cumsum_seq(x_t, tri):
    nb, h, seq = x_t.shape
    return pl.pallas_call(
        _cumsum_kernel,
        grid=(nb, seq // LANES),
        in_specs=[pl.BlockSpec((1, h, LANES), lambda b, j: (b, 0, j)), _resident(tri.shape)],
        out_specs=pl.BlockSpec((1, h, LANES), lambda b, j: (b, 0, j)),
        out_shape=jax.ShapeDtypeStruct((nb, h, seq), F32),
        scratch_shapes=[pltpu.VMEM((h, LANES), F32)],
        compiler_params=_params("parallel", "arbitrary"),
        name="cumsum",
    )(x_t, tri)


def _fox_flash_kernel(q_ref, k_ref, v_ref, c_ref, o_ref, ma, la, acca, mb, lb, accb, *, tq, tk):
    i = pl.program_id(2)
    q2 = q_ref[...]
    lane = lax.broadcasted_iota(jnp.int32, q2.shape, 1)
    zero = jnp.zeros_like(q2)
    qa = jnp.where(lane < HD_C, q2, zero)
    qb = jnp.where(lane >= HD_C, q2, zero)
    _init_state(ma, la, acca)
    _init_state(mb, lb, accb)

    def scores(j):
        off = pl.multiple_of(j * tk, tk)
        k = k_ref[pl.ds(off, tk), :]
        cb = c_ref[0, :, pl.ds(off, tk)]
        return _dot_nt(qa, k) - cb[0:1, :], _dot_nt(qb, k) - cb[1:2, :]

    def update(j, ss, masked):
        off = pl.multiple_of(j * tk, tk)
        v = v_ref[pl.ds(off, tk), :]
        if masked:
            qpos = i * tq + lax.broadcasted_iota(jnp.int32, (tq, tk), 0)
            kpos = off + lax.broadcasted_iota(jnp.int32, (tq, tk), 1)
            keep = kpos <= qpos
        for s, st in zip(ss, ((ma, la, acca), (mb, lb, accb))):
            if masked:
                s = jnp.where(keep, s, NEG_INF)
            _online_update(*st, s, lambda p: _dot(p, v))

    nfull = (i * tq) // tk

    def body(j, ss):
        nxt = scores(j + 1)
        update(j, ss, False)
        return nxt

    ss = lax.fori_loop(0, nfull, body, scores(0))
    update(nfull, ss, True)
    lane_o = lax.broadcasted_iota(jnp.int32, (tq, LANES), 1)
    o = jnp.where(lane_o < HD_C, acca[...] / la[...], accb[...] / lb[...])
    o_ref[...] = o.astype(BF16)


def _fox_prompt(qkv, c_pairs, nb, seq):
    tq = min(512, seq)
    tk = min(512, seq)
    nq = seq // tq
    npair = H_C * HD_C // LANES
    st = lambda w: pltpu.VMEM((tq, w), F32)
    return pl.pallas_call(
        functools.partial(_fox_flash_kernel, tq=tq, tk=tk),
        grid=(nb, npair, nq),
        in_specs=[pl.BlockSpec((tq, LANES), lambda b, h, i: (b * nq + i, h)),
                  pl.BlockSpec((seq, LANES), lambda b, h, i: (b, npair + h)),
                  pl.BlockSpec((seq, LANES), lambda b, h, i: (b, 2 * npair + h)),
                  pl.BlockSpec((1, 2, seq), lambda b, h, i: (b * npair + h, 0, 0))],
        out_specs=pl.BlockSpec((tq, LANES), lambda b, h, i: (b * nq + i, h)),
        out_shape=jax.ShapeDtypeStruct((nb * seq, H_C * HD_C), BF16),
        scratch_shapes=[st(1), st(1), st(LANES), st(1), st(1), st(LANES)],
        compiler_params=_params("parallel", "parallel", "arbitrary"),
        name="fox_prompt",
    )(qkv, qkv, qkv, c_pairs)


def _mla_dec_kernel(pt_ref, q_ref, *refs, pp, t_new):
    del pt_ref
    ck_refs, kp_refs = refs[:pp], refs[pp:2 * pp]
    kn_ref, o_ref, m_ref, l_ref, acc_ref = refs[2 * pp:]
    c = pl.program_id(1)

    @pl.when(c == 0)
    def _():
        _init_state(m_ref, l_ref, acc_ref)

    q = q_ref[0]
    ql, qp = q[:, :KV_RANK], q[:, KV_RANK:KV_RANK + ROPE_B]
    cks = [r[0, 0].astype(BF16) for r in ck_refs]
    s = jnp.concatenate([_dot_nt(ql, ck) + _dot(qp, kp[0, 0].astype(BF16))
                         for ck, kp in zip(cks, kp_refs)], axis=1) * MLA_SCALE

    def pv(p):
        out = None
        for n, ck in enumerate(cks):
            t = _dot(p[:, n * LANES:(n + 1) * LANES], ck)
            out = t if out is None else out + t
        return out

    _online_update(m_ref, l_ref, acc_ref, s, pv)

    @pl.when(c == pl.num_programs(1) - 1)
    def _():
        kn = kn_ref[0]
        sn = _dot_nt(q, kn) * MLA_SCALE
        tok = lax.broadcasted_iota(jnp.int32, sn.shape, 0) & (t_new - 1)
        col = lax.broadcasted_iota(jnp.int32, sn.shape, 1)
        sn = jnp.where(col <= tok, sn, NEG_INF)
        _online_update(m_ref, l_ref, acc_ref, sn, lambda p: _dot(p, kn[:, :KV_RANK]))
        o_ref[0] = acc_ref[...] / l_ref[...]


def _mla_decode(page_table, q, pool_ckv, pool_kpe, k_new, layer, t_new):
    bs, rows, kw = q.shape
    n_pages = page_table.shape[1]
    page = pool_ckv.shape[2]
    pp = min(16, n_pages)
    assert page == LANES and n_pages % pp == 0 and t_new & (t_new - 1) == 0
    nch = n_pages // pp
    pt = page_table.reshape(-1)

    def pool_spec(shape, n):
        return pl.BlockSpec((1, 1) + shape,
                            lambda b, c, pt_ref: (layer, pt_ref[b * n_pages + c * pp + n], 0, 0))

    in_specs = ([pl.BlockSpec((1, rows, kw), lambda b, c, pt_ref: (b, 0, 0))]
                + [pool_spec((page, KV_RANK), n) for n in range(pp)]
                + [pool_spec((ROPE_B, page), n) for n in range(pp)]
                + [pl.BlockSpec((1, page, kw), lambda b, c, pt_ref: (b, 0, 0))])
    return pl.pallas_call(
        functools.partial(_mla_dec_kernel, pp=pp, t_new=t_new),
        grid_spec=pltpu.PrefetchScalarGridSpec(
            num_scalar_prefetch=1, grid=(bs, nch), in_specs=in_specs,
            out_specs=pl.BlockSpec((1, rows, KV_RANK), lambda b, c, pt_ref: (b, 0, 0)),
            scratch_shapes=[pltpu.VMEM((rows, 1), F32), pltpu.VMEM((rows, 1), F32),
                            pltpu.VMEM((rows, KV_RANK), F32)]),
        out_shape=jax.ShapeDtypeStruct((bs, rows, KV_RANK), F32),
        compiler_params=_params("parallel", "arbitrary"),
        name="mla_decode",
    )(pt, q, *([pool_ckv] * pp), *([pool_kpe] * pp), k_new)


def _fox_dec_kernel(pt_ref, q_ref, *refs, pp, t_new):
    del pt_ref
    k_refs, v_refs, lf_refs = refs[:pp], refs[pp:2 * pp], refs[2 * pp:3 * pp]
    kn_ref, vn_ref, lfn_ref, tri_ref, o_ref, m_ref, l_ref, acc_ref, c_ref = refs[3 * pp:]
    c = pl.program_id(1)

    @pl.when(c == 0)
    def _():
        _init_state(m_ref, l_ref, acc_ref)
        c_ref[...] = jnp.zeros(c_ref.shape, F32)

    q = q_ref[0]
    tri = tri_ref[...]

    def scores(k_bf16, lf_t):
        cpage = _cumsum_lanes(lf_t, tri) + c_ref[...]
        c_ref[...] = jnp.broadcast_to(cpage[:, LANES - 1:LANES], c_ref.shape)
        return _dot(q, k_bf16) - jnp.concatenate([cpage] * t_new, axis=0)

    def pv_pages(vs):
        def pv(p):
            out = None
            for n, v in enumerate(vs):
                t = _dot_nt(p[:, n * LANES:(n + 1) * LANES], v)
                out = t if out is None else out + t
            return out
        return pv

    s = jnp.concatenate([scores(kr[0, 0].astype(BF16), lr[0, 0]) for kr, lr in zip(k_refs, lf_refs)], axis=1)
    _online_update(m_ref, l_ref, acc_ref, s, pv_pages([r[0, 0].astype(BF16) for r in v_refs]))

    @pl.when(c == pl.num_programs(1) - 1)
    def _():
        sn = scores(kn_ref[0], lfn_ref[0])
        tok = lax.broadcasted_iota(jnp.int32, sn.shape, 0) // H_C
        col = lax.broadcasted_iota(jnp.int32, sn.shape, 1)
        sn = jnp.where(col <= tok, sn, NEG_INF)
        _online_update(m_ref, l_ref, acc_ref, sn, pv_pages([vn_ref[0]]))
        o = acc_ref[...] / l_ref[...]
        row_h = lax.broadcasted_iota(jnp.int32, o.shape, 0) % H_C
        lane_h = lax.broadcasted_iota(jnp.int32, o.shape, 1) // HD_C
        o = jnp.where(row_h == lane_h, o, 0.0)
        o_ref[0] = jnp.sum(o.reshape(t_new, H_C, o.shape[1]), axis=1)


def _fox_decode(page_table, q_bd, pool_k_t, pool_v_t, pool_lf_t, k_new_t, v_new_t, lf_new_t, tri, layer, t_new):
    bs, rows, w = q_bd.shape
    n_pages = page_table.shape[1]
    page = pool_k_t.shape[3]
    pp = min(8, n_pages)
    assert page == LANES and n_pages % pp == 0
    nch = n_pages // pp
    pt = page_table.reshape(-1)

    def pool_spec(shape, n):
        return pl.BlockSpec((1, 1) + shape,
                            lambda b, c, pt_ref: (layer, pt_ref[b * n_pages + c * pp + n], 0, 0))

    per_seq = lambda shape: pl.BlockSpec((1,) + shape, lambda b, c, pt_ref: (b, 0, 0))
    in_specs = ([per_seq((rows, w))]
                + [pool_spec((w, page), n) for n in range(pp)] + [pool_spec((w, page), n) for n in range(pp)]
                + [pool_spec((H_C, page), n) for n in range(pp)]
                + [per_seq((w, page)), per_seq((w, page)), per_seq((H_C, page)),
                   pl.BlockSpec(tri.shape, lambda b, c, pt_ref: (0, 0))])
    return pl.pallas_call(
        functools.partial(_fox_dec_kernel, pp=pp, t_new=t_new),
        grid_spec=pltpu.PrefetchScalarGridSpec(
            num_scalar_prefetch=1, grid=(bs, nch), in_specs=in_specs,
            out_specs=per_seq((t_new, w)),
            scratch_shapes=[pltpu.VMEM((rows, 1), F32), pltpu.VMEM((rows, 1), F32),
                            pltpu.VMEM((rows, w), F32), pltpu.VMEM((H_C, LANES), F32)]),
        out_shape=jax.ShapeDtypeStruct((bs, t_new, w), F32),
        compiler_params=_params("parallel", "arbitrary"),
        name="fox_decode",
    )(pt, q_bd, *([pool_k_t] * pp), *([pool_v_t] * pp), *([pool_lf_t] * pp), k_new_t, v_new_t, lf_new_t, tri)


def _rope_tables(pos):
    pos = pos.astype(F32)[:, None]

    def cs(half):
        inv = ROPE_THETA ** (-jnp.arange(half, dtype=F32) / half)
        ang = pos * inv[None, :]
        return jnp.cos(ang), jnp.sin(ang)

    c, s = cs(DK_A // 2)
    rope_a = jnp.concatenate([c, c, -s, s], axis=1)
    c, s = cs(ROPE_B // 2)
    z = lambda w: jnp.zeros((pos.shape[0], w), F32)
    rope_b = jnp.concatenate([c, c, z(LANES - ROPE_B), -s, z(LANES - ROPE_B // 2),
                              z(ROPE_B // 2), s, z(LANES - ROPE_B)], axis=1)
    return rope_a, rope_b


def _pad_cols(w, width):
    return jnp.pad(w, ((0, 0), (0, width - w.shape[1])))


def kernel(x_prompt, x_sample, state_ret, cache_mla_ckv, cache_mla_kpe, cache_fox_k, cache_fox_v,
           cache_fox_logf, page_table, p_prompt, p_sample, norm_gains, ffn_w_in, ffn_w_out, ple_w_gate,
           ple_w_proj, w_in0, ret_gn, mla_q_norm, mla_w_uq, mla_kv_norm, mla_w_uk, mla_w_uv, w_out0,
           w_in1, fox_b_f, w_out1):
    nb, seq, d = x_prompt.shape
    bs, t_new, _ = x_sample.shape
    depth = norm_gains.shape[0]
    n_p, n_s = nb * seq, bs * t_new
    n_pages, page = page_table.shape[1], cache_mla_ckv.shape[2]
    past_len = n_pages * page
    assert seq % RET_CHUNK == 0 and t_new <= RET_CHUNK

    x = jnp.concatenate([x_prompt.reshape(n_p, d), x_sample.reshape(n_s, d)], axis=0)
    pos = jnp.concatenate([jnp.tile(jnp.arange(seq, dtype=jnp.int32), nb),
                           jnp.tile(past_len + jnp.arange(t_new, dtype=jnp.int32), bs)])
    rope_a, rope_b = _rope_tables(pos)
    eye_b = jnp.eye(H_B, dtype=F32)
    ii = jnp.arange(LANES)
    tri = (ii[:, None] <= ii[None, :]).astype(BF16)

    rets_p, rets_s, ckvs, kpes, fks, fvs, fls = [], [], [], [], [], [], []
    for i in range(depth):
        g = norm_gains[i]
        j = i // 2
        x = _ffn(x, g, ffn_w_in[i, 0].astype(BF16), ffn_w_out[i, 0].astype(BF16), 0)
        if i % 2 == 0:
            w_in = _pad_cols(w_in0[j], IN0_PAD).astype(BF16)
            uq = mla_w_uq[j]
            w_uq = jnp.concatenate(
                [uq[:, :, :NOPE_B].reshape(Q_RANK, H_B * NOPE_B),
                 jnp.pad(uq[:, :, NOPE_B:], ((0, 0), (0, 0), (0, LANES - ROPE_B))).reshape(Q_RANK, H_B * LANES)],
                axis=1).astype(BF16)
            w_bd = jnp.einsum('hdr,hg->hdgr', jnp.transpose(mla_w_uk[j], (1, 2, 0)), eye_b)
            w_bd = w_bd.reshape(H_B * NOPE_B, H_B * KV_RANK).astype(BF16)
            w_uv = jnp.einsum('rhe,hg->hrge', mla_w_uv[j], eye_b).reshape(H_B, KV_RANK, H_B * VD_B).astype(BF16)
            ret, qc, kc, ckv, kpe = _even_in(x, g, w_in, rope_a, rope_b, mla_q_norm[j][None], mla_kv_norm[j][None],
                                             w_uq, w_bd)
            wq = 3 * H_A * DK_A
            o_rp, s_p = _retention(ret, jnp.zeros((nb, H_A, DK_A, DV_A), F32), nb, seq // RET_CHUNK, float(RET_CHUNK))
            ret_s = jnp.pad(ret[n_p:, :wq].reshape(bs, t_new, wq), ((0, 0), (0, RET_CHUNK - t_new), (0, 0)))
            o_rs, s_s = _retention(ret_s.reshape(bs * RET_CHUNK, wq), state_ret[j], bs, 1, float(t_new))
            o_rs = o_rs.reshape(bs, RET_CHUNK, H_A * DV_A)[:, :t_new].reshape(n_s, H_A * DV_A)
            o_r = jnp.concatenate([o_rp, o_rs], axis=0)
            ol_p = _mla_prompt(qc, kc, nb, seq)
            kw = kc.shape[1]
            q_s = qc[:, n_p:].reshape(H_B, bs, t_new, kw).transpose(1, 0, 2, 3).reshape(bs, H_B * t_new, kw)
            k_new = jnp.pad(kc[n_p:].reshape(bs, t_new, kw), ((0, 0), (0, page - t_new), (0, 0)))
            ol_s = _mla_decode(page_table, q_s, cache_mla_ckv, cache_mla_kpe.transpose(0, 1, 3, 2), k_new, j, t_new)
            ol_s = ol_s.reshape(bs, H_B, t_new, KV_RANK).transpose(1, 0, 2, 3).reshape(H_B, n_s, KV_RANK)
            o_lat = jnp.concatenate([ol_p, ol_s.astype(BF16)], axis=1)
            x = _even_out(x, o_r, ret, o_lat, g, ret_gn[j].reshape(1, H_A * DV_A), w_uv, w_out0[j].astype(BF16))
            rets_p.append(s_p)
            rets_s.append(s_s)
            ckvs.append(ckv)
            kpes.append(kpe)
        else:
            w = H_C * HD_C
            w_in = _pad_cols(w_in1[j], IN1_PAD).astype(BF16)
            b_f = jnp.pad(fox_b_f[j], (0, LANES - H_C))[None]
            qkv, fk, fv, fl = _odd_in(x, g, w_in, b_f)
            c_t = _cumsum_seq(fl[:n_p].reshape(nb, seq, H_C).transpose(0, 2, 1), tri)
            o_p = _fox_prompt(qkv, c_t.reshape(nb * H_C // 2, 2, seq), nb, seq)
            q4 = qkv[n_p:, :w].reshape(bs, t_new, H_C, 1, HD_C)
            q_bd = (q4 * jnp.eye(H_C, dtype=BF16)[None, None, :, :, None]).reshape(bs, t_new * H_C, w)
            keys_last = lambda a, c: jnp.pad(a.reshape(bs, t_new, c).transpose(0, 2, 1),
                                             ((0, 0), (0, 0), (0, page - t_new)))
            n_pool = cache_fox_k.shape[1]
            pool_t = lambda a: a.transpose(0, 1, 3, 4, 2).reshape(-1, n_pool, w, page)
            o_s = _fox_decode(page_table, q_bd, pool_t(cache_fox_k), pool_t(cache_fox_v),
                              cache_fox_logf.transpose(0, 1, 3, 2), keys_last(qkv[n_p:, w:2 * w], w),
                              keys_last(qkv[n_p:, 2 * w:], w), keys_last(fl[n_p:], H_C), tri, j, t_new)
            a = jnp.concatenate([o_p, o_s.reshape(n_s, w).astype(BF16)], axis=0)
            x = _odd_out(x, a, g, w_out1[j].astype(BF16))
            fks.append(fk)
            fvs.append(fv)
            fls.append(fl)
        x = _ffn(x, g, ffn_w_in[i, 1].astype(BF16), ffn_w_out[i, 1].astype(BF16), 4)
        p = jnp.concatenate([p_prompt[i].reshape(n_p, -1), p_sample[i].reshape(n_s, -1)], axis=0)
        x = _ple(x, p, g, ple_w_gate[i].astype(BF16), ple_w_proj[i].astype(BF16))

    def split(parts, tail):
        a = jnp.stack(parts)
        return (a[:, :n_p].reshape((len(parts), nb, seq) + tail),
                a[:, n_p:].reshape((len(parts), bs, t_new) + tail))

    ckv_p, ckv_s = split(ckvs, (KV_RANK,))
    kpe_p, kpe_s = split(kpes, (ROPE_B,))
    fk_p, fk_s = split(fks, (H_C, HD_C))
    fv_p, fv_s = split(fvs, (H_C, HD_C))
    fl_p, fl_s = split(fls, (H_C,))
    return (x[:n_p].reshape(nb, seq, d), x[n_p:].reshape(bs, t_new, d), jnp.stack(rets_p), jnp.stack(rets_s),
            ckv_p, ckv_s, kpe_p, kpe_s, fk_p, fk_s, fv_p, fv_s, fl_p, fl_s)
```

```python
import functools

import jax
import jax.numpy as jnp
import numpy as np
from jax import lax
from jax.experimental import pallas as pl
from jax.experimental.pallas import tpu as pltpu

F32 = jnp.float32
BF16 = jnp.bfloat16

H_A, DK_A, DV_A = 4, 128, 128
RET_CHUNK = 128
H_B, NOPE_B, ROPE_B, VD_B = 8, 64, 32, 64
Q_RANK, KV_RANK = 256, 128
H_C, HD_C = 16, 64
ROPE_THETA = 10000.0
NORM_EPS = 1e-6
NEG_INF = -1e30
MLA_SCALE = (NOPE_B + ROPE_B) ** -0.5
FOX_SCALE = HD_C ** -0.5
LOG2E = 1.4426950408889634

LANES = 128
IN0_PAD = 2560
IN1_PAD = 3200
VMEM_LIMIT = 56 * 1024 * 1024

_NT = (((1,), (1,)), ((), ()))
_TN = (((0,), (0,)), ((), ()))


def _dot(a, b):
    return jnp.dot(a, b, preferred_element_type=F32)


def _dot_nt(a, b):
    return lax.dot_general(a, b, _NT, preferred_element_type=F32)


def _dot_tn(a, b):
    return lax.dot_general(a, b, _TN, preferred_element_type=F32)


def _rms(x, g):
    return x * lax.rsqrt(jnp.mean(x * x, axis=-1, keepdims=True) + NORM_EPS) * g


def _row_tile(n, cap=512):
    t = cap
    while n % t:
        t //= 2
    assert t >= 8, n
    return t


def _params(*sem):
    return pltpu.CompilerParams(dimension_semantics=sem, vmem_limit_bytes=VMEM_LIMIT)


def _resident(shape):
    nd = len(shape)
    return pl.BlockSpec(shape, lambda *_: (0,) * nd, pipeline_mode=pl.Buffered(1))


def _ffn_body(x, g_pre, g_post, win_ref, wout_ref, dff, chunk):
    h = _rms(x, g_pre).astype(BF16)
    acc = jnp.zeros(x.shape, F32)
    for c0 in range(0, dff, chunk):
        a = _dot(h, win_ref[:, c0:c0 + chunk])
        b = _dot(h, win_ref[:, dff + c0:dff + c0 + chunk])
        act = (a * jax.nn.sigmoid(a)) * b
        acc = acc + _dot(act.astype(BF16), wout_ref[c0:c0 + chunk, :])
    return x + 0.5 * _rms(acc, g_post)


def _ffn_kernel(x_ref, g_ref, win_ref, wout_ref, o_ref, *, dff, chunk, gi):
    o_ref[...] = _ffn_body(x_ref[...], g_ref[gi:gi + 1], g_ref[gi + 1:gi + 2], win_ref, wout_ref, dff, chunk)


def _ffn(x, gains, w_in, w_out, gi):
    n, d = x.shape
    dff = w_out.shape[0]
    chunk = 256 if dff % 256 == 0 else LANES
    tm = _row_tile(n)
    return pl.pallas_call(
        functools.partial(_ffn_kernel, dff=dff, chunk=chunk, gi=gi),
        grid=(n // tm,),
        in_specs=[pl.BlockSpec((tm, d), lambda i: (i, 0)), _resident(gains.shape),
                  _resident(w_in.shape), _resident(w_out.shape)],
        out_specs=pl.BlockSpec((tm, d), lambda i: (i, 0)),
        out_shape=jax.ShapeDtypeStruct((n, d), F32),
        compiler_params=_params("parallel"),
        name="ffn",
    )(x, gains, w_in, w_out)


def _ple_kernel(x_ref, p_ref, g_ref, wg_ref, wp_ref, o_ref):
    x = x_ref[...]
    gate = jax.nn.sigmoid(_dot(_rms(x, g_ref[6:7]).astype(BF16), wg_ref[...]))
    e = _dot(p_ref[...].astype(BF16), wp_ref[...])
    o_ref[...] = x + _rms(gate * e, g_ref[7:8])


def _ple(x, p, gains, w_gate, w_proj):
    n, d = x.shape
    tm = _row_tile(n)
    return pl.pallas_call(
        _ple_kernel,
        grid=(n // tm,),
        in_specs=[pl.BlockSpec((tm, d), lambda i: (i, 0)), pl.BlockSpec((tm, p.shape[1]), lambda i: (i, 0)),
                  _resident(gains.shape), _resident(w_gate.shape), _resident(w_proj.shape)],
        out_specs=pl.BlockSpec((tm, d), lambda i: (i, 0)),
        out_shape=jax.ShapeDtypeStruct((n, d), F32),
        compiler_params=_params("parallel"),
        name="ple",
    )(x, p, gains, w_gate, w_proj)


def _even_in_kernel(x_ref, g_ref, win_ref, ra_ref, rb_ref, qn_ref, kvn_ref, wuq_ref, wbd_ref,
                    ret_ref, qc_ref, kc_ref, ckv_ref, kpe_ref):
    h = _rms(x_ref[...], g_ref[2:3]).astype(BF16)
    z = _dot(h, win_ref[...])
    hw = H_A * DK_A
    cosr, sinr = ra_ref[:, 0:LANES], ra_ref[:, LANES:2 * LANES]
    for hh in range(H_A):
        for base, sc in ((0, None), (hw, DK_A ** -0.5)):
            xh = z[:, base + hh * DK_A:base + (hh + 1) * DK_A]
            r = xh * cosr + pltpu.roll(xh, DK_A // 2, 1) * sinr
            ret_ref[:, base + hh * DK_A:base + (hh + 1) * DK_A] = r if sc is None else r * sc
    ret_ref[:, 2 * hw:4 * hw] = z[:, 2 * hw:4 * hw]

    c_cos, s_lo, s_hi = rb_ref[:, 0:LANES], rb_ref[:, LANES:2 * LANES], rb_ref[:, 2 * LANES:3 * LANES]

    def rope32(v):
        return (v * c_cos + pltpu.roll(v, LANES - ROPE_B // 2, 1) * s_lo
                + pltpu.roll(v, ROPE_B // 2, 1) * s_hi)

    o = 4 * hw
    cq = _rms(z[:, o:o + Q_RANK], qn_ref[...]).astype(BF16)
    q = _dot(cq, wuq_ref[...])
    nq = H_B * NOPE_B
    qlat = _dot(q[:, :nq].astype(BF16), wbd_ref[...])
    ckv = _rms(z[:, o + Q_RANK:o + Q_RANK + KV_RANK], kvn_ref[...])
    kpe = rope32(z[:, o + Q_RANK + KV_RANK:o + Q_RANK + KV_RANK + LANES])
    ckv_ref[...] = ckv
    kpe_ref[...] = kpe[:, :ROPE_B]
    kc_ref[:, 0:KV_RANK] = ckv.astype(BF16)
    kc_ref[:, KV_RANK:KV_RANK + LANES] = kpe.astype(BF16)
    for hh in range(H_B):
        qc_ref[hh, :, 0:KV_RANK] = qlat[:, hh * KV_RANK:(hh + 1) * KV_RANK].astype(BF16)
        qc_ref[hh, :, KV_RANK:KV_RANK + LANES] = rope32(q[:, nq + hh * LANES:nq + (hh + 1) * LANES]).astype(BF16)


def _even_in(x, gains, w_in, rope_a, rope_b, qn, kvn, w_uq, w_bd):
    n, d = x.shape
    tm = _row_tile(n)
    kw = KV_RANK + LANES
    row = lambda w: pl.BlockSpec((tm, w), lambda i: (i, 0))
    return pl.pallas_call(
        _even_in_kernel,
        grid=(n // tm,),
        in_specs=[row(d), _resident(gains.shape), _resident(w_in.shape), row(rope_a.shape[1]),
                  row(rope_b.shape[1]), _resident(qn.shape), _resident(kvn.shape), _resident(w_uq.shape),
                  _resident(w_bd.shape)],
        out_specs=[row(4 * H_A * DK_A), pl.BlockSpec((H_B, tm, kw), lambda i: (0, i, 0)), row(kw),
                   row(KV_RANK), row(ROPE_B)],
        out_shape=[jax.ShapeDtypeStruct((n, 4 * H_A * DK_A), F32), jax.ShapeDtypeStruct((H_B, n, kw), BF16),
                   jax.ShapeDtypeStruct((n, kw), BF16), jax.ShapeDtypeStruct((n, KV_RANK), F32),
                   jax.ShapeDtypeStruct((n, ROPE_B), F32)],
        compiler_params=_params("parallel"),
        name="even_in",
    )(x, gains, w_in, rope_a, rope_b, qn, kvn, w_uq, w_bd)


def _even_out_kernel(x_ref, or_ref, gr_ref, ol_ref, g_ref, gn_ref, wuv_ref, wout_ref, o_ref):
    hw = H_A * DV_A
    gate = gr_ref[...]
    gate = gate * jax.nn.sigmoid(gate)
    y = None
    for hh in range(H_A):
        sl = slice(hh * DV_A, (hh + 1) * DV_A)
        oh = or_ref[:, sl]
        xc = oh - jnp.mean(oh, axis=-1, keepdims=True)
        var = jnp.mean(xc * xc, axis=-1, keepdims=True)
        a = gate[:, sl] * (xc * lax.rsqrt(var + NORM_EPS) * gn_ref[:, sl])
        t = _dot(a.astype(BF16), wout_ref[sl, :])
        y = t if y is None else y + t
    om = None
    for hh in range(H_B):
        t = _dot(ol_ref[hh], wuv_ref[hh])
        om = t if om is None else om + t
    y = y + _dot(om.astype(BF16), wout_ref[hw:, :])
    o_ref[...] = x_ref[...] + _rms(y, g_ref[3:4])


def _even_out(x, o_r, ret, o_lat, gains, gn, w_uv, w_out):
    n, d = x.shape
    tm = _row_tile(n)
    hw = H_A * DV_A
    return pl.pallas_call(
        _even_out_kernel,
        grid=(n // tm,),
        in_specs=[pl.BlockSpec((tm, d), lambda i: (i, 0)), pl.BlockSpec((tm, hw), lambda i: (i, 0)),
                  pl.BlockSpec((tm, hw), lambda i: (i, 3)),
                  pl.BlockSpec((H_B, tm, KV_RANK), lambda i: (0, i, 0)),
                  _resident(gains.shape), _resident(gn.shape), _resident(w_uv.shape), _resident(w_out.shape)],
        out_specs=pl.BlockSpec((tm, d), lambda i: (i, 0)),
        out_shape=jax.ShapeDtypeStruct((n, d), F32),
        compiler_params=_params("parallel"),
        name="even_out",
    )(x, o_r, ret, o_lat, gains, gn, w_uv, w_out)


def _odd_in_kernel(x_ref, g_ref, win_ref, bf_ref, qkv_ref, fk_ref, fv_ref, fl_ref):
    h = _rms(x_ref[...], g_ref[2:3]).astype(BF16)
    z = _dot(h, win_ref[...])
    w = H_C * HD_C
    qkv_ref[:, 0:w] = (z[:, 0:w] * FOX_SCALE).astype(BF16)
    qkv_ref[:, w:3 * w] = z[:, w:3 * w].astype(BF16)
    fk_ref[...] = z[:, w:2 * w]
    fv_ref[...] = z[:, 2 * w:3 * w]
    t = z[:, 3 * w:3 * w + LANES] + bf_ref[...]
    ls = jnp.minimum(t, 0.0) - jnp.log1p(jnp.exp(-jnp.abs(t)))
    fl_ref[...] = ls[:, :H_C]


def _odd_in(x, gains, w_in, b_f):
    n, d = x.shape
    tm = _row_tile(n)
    w = H_C * HD_C
    row = lambda c: pl.BlockSpec((tm, c), lambda i: (i, 0))
    return pl.pallas_call(
        _odd_in_kernel,
        grid=(n // tm,),
        in_specs=[row(d), _resident(gains.shape), _resident(w_in.shape), _resident(b_f.shape)],
        out_specs=[row(3 * w), row(w), row(w), row(H_C)],
        out_shape=[jax.ShapeDtypeStruct((n, 3 * w), BF16), jax.ShapeDtypeStruct((n, w), F32),
                   jax.ShapeDtypeStruct((n, w), F32), jax.ShapeDtypeStruct((n, H_C), F32)],
        compiler_params=_params("parallel"),
        name="odd_in",
    )(x, gains, w_in, b_f)


def _odd_out_kernel(x_ref, a_ref, g_ref, wout_ref, o_ref):
    o_ref[...] = x_ref[...] + _rms(_dot(a_ref[...], wout_ref[...]), g_ref[3:4])


def _odd_out(x, a, gains, w_out):
    n, d = x.shape
    tm = _row_tile(n)
    return pl.pallas_call(
        _odd_out_kernel,
        grid=(n // tm,),
        in_specs=[pl.BlockSpec((tm, d), lambda i: (i, 0)), pl.BlockSpec((tm, a.shape[1]), lambda i: (i, 0)),
                  _resident(gains.shape), _resident(w_out.shape)],
        out_specs=pl.BlockSpec((tm, d), lambda i: (i, 0)),
        out_shape=jax.ShapeDtypeStruct((n, d), F32),
        compiler_params=_params("parallel"),
        name="odd_out",
    )(x, a, gains, w_out)


def _ret_kernel(qkv_ref, s0_ref, dec_ref, qd_ref, kd_ref, sd_ref, o_ref, s_ref):
    @pl.when(pl.program_id(1) == 0)
    def _():
        s_ref[...] = s0_ref[...]

    for hh in range(H_A):
        q = qkv_ref[:, hh * DK_A:(hh + 1) * DK_A]
        k = qkv_ref[:, (H_A + hh) * DK_A:(H_A + hh + 1) * DK_A]
        v = qkv_ref[:, 2 * H_A * DK_A + hh * DV_A:2 * H_A * DK_A + (hh + 1) * DV_A].astype(BF16)
        s = s_ref[0, hh]
        qb = q.astype(BF16)
        scores = _dot_nt(qb, k.astype(BF16)) * dec_ref[hh]
        o = _dot(scores.astype(BF16), v) + _dot(qb, s.astype(BF16)) * qd_ref[hh]
        o_ref[:, hh * DV_A:(hh + 1) * DV_A] = o
        s_ref[0, hh] = sd_ref[hh] * s + _dot_tn((k * kd_ref[hh]).astype(BF16), v)


def _retention(qkv, s0, nb, nc, length):
    lp = RET_CHUNK
    log_gamma = jnp.log1p(-jnp.exp2(-5.0 - jnp.arange(H_A, dtype=F32)))
    idx = jnp.arange(lp, dtype=F32)
    diff = idx[:, None] - idx[None, :]
    decay = jnp.where(diff >= 0, jnp.exp(log_gamma[:, None, None] * jnp.maximum(diff, 0.0)), 0.0)
    ones = jnp.ones((1, 1, DV_A), F32)
    qd = jnp.exp(log_gamma[:, None] * (idx[None, :] + 1.0))[:, :, None] * ones
    kd = jnp.exp(log_gamma[:, None] * (length - 1.0 - idx[None, :]))[:, :, None] * ones
    sd = jnp.exp(log_gamma * length)[:, None, None] * ones
    wq = 3 * H_A * DK_A
    return pl.pallas_call(
        _ret_kernel,
        grid=(nb, nc),
        in_specs=[pl.BlockSpec((lp, wq), lambda b, c: (b * nc + c, 0)),
                  pl.BlockSpec((1, H_A, DK_A, DV_A), lambda b, c: (b, 0, 0, 0)),
                  _resident(decay.shape), _resident(qd.shape), _resident(kd.shape), _resident(sd.shape)],
        out_specs=[pl.BlockSpec((lp, H_A * DV_A), lambda b, c: (b * nc + c, 0)),
                   pl.BlockSpec((1, H_A, DK_A, DV_A), lambda b, c: (b, 0, 0, 0))],
        out_shape=[jax.ShapeDtypeStruct((nb * nc * lp, H_A * DV_A), F32),
                   jax.ShapeDtypeStruct((nb, H_A, DK_A, DV_A), F32)],
        compiler_params=_params("parallel", "arbitrary"),
        name="retention",
    )(qkv, s0, decay, qd, kd, sd)


def _online_update(m_ref, l_ref, acc_ref, s, pv):
    m_prev = m_ref[...]
    m_new = jnp.maximum(m_prev, jnp.max(s, axis=1, keepdims=True))
    alpha = jnp.exp(m_prev - m_new)
    p = jnp.exp(s - m_new)
    l_ref[...] = alpha * l_ref[...] + jnp.sum(p, axis=1, keepdims=True)
    acc_ref[...] = alpha * acc_ref[...] + pv(p.astype(BF16))
    m_ref[...] = m_new


def _init_state(m_ref, l_ref, acc_ref):
    m_ref[...] = jnp.full(m_ref.shape, NEG_INF, F32)
    l_ref[...] = jnp.zeros(l_ref.shape, F32)
    acc_ref[...] = jnp.zeros(acc_ref.shape, F32)


def _mla_flash_kernel(q_ref, k_ref, o_ref, m_ref, acc_ref, *, tq, tk):
    i = pl.program_id(1)
    rows = H_B * tq
    q = q_ref[...].reshape(rows, q_ref.shape[2])
    m_ref[...] = jnp.full(m_ref.shape, NEG_INF, F32)
    acc_ref[...] = jnp.zeros(acc_ref.shape, F32)
    ones = jnp.ones((tk, LANES), BF16)
    c = MLA_SCALE * LOG2E

    def scores(j):
        return _dot_nt(q, k_ref[pl.ds(pl.multiple_of(j * tk, tk), tk), :])

    def update(j, s, masked):
        off = pl.multiple_of(j * tk, tk)
        if masked:
            qpos = i * tq + (lax.broadcasted_iota(jnp.int32, (rows, tk), 0) & (tq - 1))
            kpos = off + lax.broadcasted_iota(jnp.int32, (rows, tk), 1)
            s = jnp.where(kpos <= qpos, s, NEG_INF)
        m_prev = m_ref[...]
        m_new = jnp.maximum(m_prev, jnp.max(s, axis=1, keepdims=True))
        p = jnp.exp2(((s - m_new) * c).astype(BF16))
        v = jnp.concatenate([k_ref[pl.ds(off, tk), 0:KV_RANK], ones], axis=1)
        acc_ref[...] = jnp.exp2((m_prev - m_new) * c) * acc_ref[...] + _dot(p, v)
        m_ref[...] = m_new

    nfull = (i * tq) // tk

    def body(j, s):
        nxt = scores(j + 1)
        update(j, s, False)
        return nxt

    update(nfull, lax.fori_loop(0, nfull, body, scores(0)), True)
    o = acc_ref[:, 0:KV_RANK] / acc_ref[:, KV_RANK:KV_RANK + LANES]
    o_ref[...] = o.reshape(H_B, tq, KV_RANK).astype(BF16)


def _mla_prompt(qc, kc, nb, seq):
    tq = min(128, seq)
    tk = min(512, seq)
    nq = seq // tq
    kw = kc.shape[1]
    return pl.pallas_call(
        functools.partial(_mla_flash_kernel, tq=tq, tk=tk),
        grid=(nb, nq),
        in_specs=[pl.BlockSpec((H_B, tq, kw), lambda b, i: (0, b * nq + i, 0)),
                  pl.BlockSpec((seq, kw), lambda b, i: (b, 0))],
        out_specs=pl.BlockSpec((H_B, tq, KV_RANK), lambda b, i: (0, b * nq + i, 0)),
        out_shape=jax.ShapeDtypeStruct((H_B, nb * seq, KV_RANK), BF16),
        scratch_shapes=[pltpu.VMEM((H_B * tq, 1), F32), pltpu.VMEM((H_B * tq, KV_RANK + LANES), F32)],
        compiler_params=_params("parallel", "arbitrary"),
        name="mla_prompt",
    )(qc, kc)


def _cumsum_lanes(x, tri):
    hi = x.astype(BF16)
    r = x - hi.astype(F32)
    mid = r.astype(BF16)
    lo = (r - mid.astype(F32)).astype(BF16)
    cc = _dot(jnp.concatenate([hi, mid, lo], axis=0), tri)
    n = x.shape[0]
    return cc[0:n] + cc[n:2 * n] + cc[2 * n:3 * n]


def _cumsum_kernel(x_ref, tri_ref, o_ref, c_ref):
    @pl.when(pl.program_id(1) == 0)
    def _():
        c_ref[...] = jnp.zeros(c_ref.shape, F32)

    c = _cumsum_lanes(x_ref[0], tri_ref[...]) + c_ref[...]
    o_ref[0] = c
    c_ref[...] = jnp.broadcast_to(c[:, LANES - 1:LANES], c_ref.shape)


def _cumsum_seq(x_t, tri):
    nb, h, seq = x_t.shape
    return pl.pallas_call(
        _cumsum_kernel,
        grid=(nb, seq // LANES),
        in_specs=[pl.BlockSpec((1, h, LANES), lambda b, j: (b, 0, j)), _resident(tri.shape)],
        out_specs=pl.BlockSpec((1, h, LANES), lambda b, j: (b, 0, j)),
        out_shape=jax.ShapeDtypeStruct((nb, h, seq), F32),
        scratch_shapes=[pltpu.VMEM((h, LANES), F32)],
        compiler_params=_params("parallel", "arbitrary"),
        name="cumsum",
    )(x_t, tri)


def _fox_flash_kernel(q_ref, k_ref, v_ref, c_ref, o_ref, ma, acca, mb, accb, *, tq, tk):
    i = pl.program_id(2)
    q2 = q_ref[...]
    lane = lax.broadcasted_iota(jnp.int32, q2.shape, 1)
    zero = jnp.zeros_like(q2)
    qa = jnp.where(lane < HD_C, q2, zero)
    qb = jnp.where(lane >= HD_C, q2, zero)
    for m_ref, acc_ref in ((ma, acca), (mb, accb)):
        m_ref[...] = jnp.full(m_ref.shape, NEG_INF, F32)
        acc_ref[...] = jnp.zeros(acc_ref.shape, F32)
    lane_v = lax.broadcasted_iota(jnp.int32, (tk, LANES), 1)
    one = jnp.ones((tk, LANES), BF16)

    def scores(j):
        off = pl.multiple_of(j * tk, tk)
        k = k_ref[pl.ds(off, tk), :]
        cb = c_ref[0, :, pl.ds(off, tk)]
        return _dot_nt(qa, k) - cb[0:1, :], _dot_nt(qb, k) - cb[1:2, :]

    def update(j, ss, masked):
        off = pl.multiple_of(j * tk, tk)
        v = v_ref[pl.ds(off, tk), :]
        vs = (jnp.where(lane_v < HD_C, v, one), jnp.where(lane_v >= HD_C, v, one))
        if masked:
            qpos = i * tq + lax.broadcasted_iota(jnp.int32, (tq, tk), 0)
            kpos = off + lax.broadcasted_iota(jnp.int32, (tq, tk), 1)
            keep = kpos <= qpos
        for s, vx, m_ref, acc_ref in zip(ss, vs, (ma, mb), (acca, accb)):
            if masked:
                s = jnp.where(keep, s, NEG_INF)
            m_prev = m_ref[...]
            m_new = jnp.maximum(m_prev, jnp.max(s, axis=1, keepdims=True))
            p = jnp.exp((s - m_new).astype(BF16))
            acc_ref[...] = jnp.exp(m_prev - m_new) * acc_ref[...] + _dot(p, vx)
            m_ref[...] = m_new

    nfull = (i * tq) // tk

    def body(j, ss):
        nxt = scores(j + 1)
        update(j, ss, False)
        return nxt

    ss = lax.fori_loop(0, nfull, body, scores(0))
    update(nfull, ss, True)
    a, b = acca[...], accb[...]
    lane_o = lax.broadcasted_iota(jnp.int32, (tq, LANES), 1)
    o = jnp.where(lane_o < HD_C, a / pltpu.roll(a, HD_C, 1), b / pltpu.roll(b, HD_C, 1))
    o_ref[...] = o.astype(BF16)


def _fox_prompt(qkv, c_pairs, nb, seq):
    tq = min(512, seq)
    tk = min(512, seq)
    nq = seq // tq
    npair = H_C * HD_C // LANES
    st = lambda w: pltpu.VMEM((tq, w), F32)
    return pl.pallas_call(
        functools.partial(_fox_flash_kernel, tq=tq, tk=tk),
        grid=(nb, npair, nq),
        in_specs=[pl.BlockSpec((tq, LANES), lambda b, h, i: (b * nq + i, h)),
                  pl.BlockSpec((seq, LANES), lambda b, h, i: (b, npair + h)),
                  pl.BlockSpec((seq, LANES), lambda b, h, i: (b, 2 * npair + h)),
                  pl.BlockSpec((1, 2, seq), lambda b, h, i: (b * npair + h, 0, 0))],
        out_specs=pl.BlockSpec((tq, LANES), lambda b, h, i: (b * nq + i, h)),
        out_shape=jax.ShapeDtypeStruct((nb * seq, H_C * HD_C), BF16),
        scratch_shapes=[st(1), st(LANES), st(1), st(LANES)],
        compiler_params=_params("parallel", "parallel", "arbitrary"),
        name="fox_prompt",
    )(qkv, qkv, qkv, c_pairs)


def _mla_dec_kernel(pt_ref, q_ref, *refs, pp, t_new):
    del pt_ref
    ck_refs, kp_refs = refs[:pp], refs[pp:2 * pp]
    kn_ref, o_ref, m_ref, l_ref, acc_ref = refs[2 * pp:]
    c = pl.program_id(1)

    @pl.when(c == 0)
    def _():
        _init_state(m_ref, l_ref, acc_ref)

    q = q_ref[0]
    ql, qp = q[:, :KV_RANK], q[:, KV_RANK:KV_RANK + ROPE_B]
    cks = [r[0, 0].astype(BF16) for r in ck_refs]
    s = jnp.concatenate([_dot_nt(ql, ck) + _dot(qp, kp[0, 0].astype(BF16))
                         for ck, kp in zip(cks, kp_refs)], axis=1) * MLA_SCALE

    def pv(p):
        out = None
        for n, ck in enumerate(cks):
            t = _dot(p[:, n * LANES:(n + 1) * LANES], ck)
            out = t if out is None else out + t
        return out

    _online_update(m_ref, l_ref, acc_ref, s, pv)

    @pl.when(c == pl.num_programs(1) - 1)
    def _():
        kn = kn_ref[0]
        sn = _dot_nt(q, kn) * MLA_SCALE
        tok = lax.broadcasted_iota(jnp.int32, sn.shape, 0) & (t_new - 1)
        col = lax.broadcasted_iota(jnp.int32, sn.shape, 1)
        sn = jnp.where(col <= tok, sn, NEG_INF)
        _online_update(m_ref, l_ref, acc_ref, sn, lambda p: _dot(p, kn[:, :KV_RANK]))
        o_ref[0] = acc_ref[...] / l_ref[...]


def _mla_decode(page_table, q, pool_ckv, pool_kpe, k_new, layer, t_new):
    bs, rows, kw = q.shape
    n_pages = page_table.shape[1]
    page = pool_ckv.shape[2]
    pp = min(16, n_pages)
    assert page == LANES and n_pages % pp == 0 and t_new & (t_new - 1) == 0
    nch = n_pages // pp
    pt = page_table.reshape(-1)

    def pool_spec(shape, n):
        return pl.BlockSpec((1, 1) + shape,
                            lambda b, c, pt_ref: (layer, pt_ref[b * n_pages + c * pp + n], 0, 0))

    in_specs = ([pl.BlockSpec((1, rows, kw), lambda b, c, pt_ref: (b, 0, 0))]
                + [pool_spec((page, KV_RANK), n) for n in range(pp)]
                + [pool_spec((ROPE_B, page), n) for n in range(pp)]
                + [pl.BlockSpec((1, page, kw), lambda b, c, pt_ref: (b, 0, 0))])
    return pl.pallas_call(
        functools.partial(_mla_dec_kernel, pp=pp, t_new=t_new),
        grid_spec=pltpu.PrefetchScalarGridSpec(
            num_scalar_prefetch=1, grid=(bs, nch), in_specs=in_specs,
            out_specs=pl.BlockSpec((1, rows, KV_RANK), lambda b, c, pt_ref: (b, 0, 0)),
            scratch_shapes=[pltpu.VMEM((rows, 1), F32), pltpu.VMEM((rows, 1), F32),
                            pltpu.VMEM((rows, KV_RANK), F32)]),
        out_shape=jax.ShapeDtypeStruct((bs, rows, KV_RANK), F32),
        compiler_params=_params("parallel", "arbitrary"),
        name="mla_decode",
    )(pt, q, *([pool_ckv] * pp), *([pool_kpe] * pp), k_new)


def _fox_dec_kernel(pt_ref, q_ref, *refs, pp, t_new):
    del pt_ref
    k_refs, v_refs, lf_refs = refs[:pp], refs[pp:2 * pp], refs[2 * pp:3 * pp]
    kn_ref, vn_ref, lfn_ref, tri_ref, o_ref, m_ref, l_ref, acc_ref, c_ref = refs[3 * pp:]
    c = pl.program_id(1)

    @pl.when(c == 0)
    def _():
        _init_state(m_ref, l_ref, acc_ref)
        c_ref[...] = jnp.zeros(c_ref.shape, F32)

    q = q_ref[0]
    tri = tri_ref[...]

    def scores(k_bf16, cpage):
        return _dot(q, k_bf16) - jnp.concatenate([cpage] * t_new, axis=0)

    def prefixes(lf_pages):
        local = _cumsum_lanes(jnp.concatenate(lf_pages, axis=0), tri)
        carry = c_ref[...]
        out = []
        for n in range(len(lf_pages)):
            cpage = local[n * H_C:(n + 1) * H_C] + carry
            carry = jnp.broadcast_to(cpage[:, LANES - 1:LANES], carry.shape)
            out.append(cpage)
        c_ref[...] = carry
        return out

    def pv_pages(vs):
        def pv(p):
            out = None
            for n, v in enumerate(vs):
                t = _dot_nt(p[:, n * LANES:(n + 1) * LANES], v)
                out = t if out is None else out + t
            return out
        return pv

    cps = prefixes([r[0, 0] for r in lf_refs])
    s = jnp.concatenate([scores(kr[0, 0].astype(BF16), cp) for kr, cp in zip(k_refs, cps)], axis=1)
    _online_update(m_ref, l_ref, acc_ref, s, pv_pages([r[0, 0].astype(BF16) for r in v_refs]))

    @pl.when(c == pl.num_programs(1) - 1)
    def _():
        sn = scores(kn_ref[0], prefixes([lfn_ref[0]])[0])
        tok = lax.broadcasted_iota(jnp.int32, sn.shape, 0) // H_C
        col = lax.broadcasted_iota(jnp.int32, sn.shape, 1)
        sn = jnp.where(col <= tok, sn, NEG_INF)
        _online_update(m_ref, l_ref, acc_ref, sn, pv_pages([vn_ref[0]]))
        o = acc_ref[...] / l_ref[...]
        row_h = lax.broadcasted_iota(jnp.int32, o.shape, 0) % H_C
        lane_h = lax.broadcasted_iota(jnp.int32, o.shape, 1) // HD_C
        o = jnp.where(row_h == lane_h, o, 0.0)
        o_ref[0] = jnp.sum(o.reshape(t_new, H_C, o.shape[1]), axis=1)


def _fox_decode(page_table, q_bd, pool_k_t, pool_v_t, pool_lf_t, k_new_t, v_new_t, lf_new_t, tri, layer, t_new):
    bs, rows, w = q_bd.shape
    n_pages = page_table.shape[1]
    page = pool_k_t.shape[3]
    pp = min(8, n_pages)
    assert page == LANES and n_pages % pp == 0
    nch = n_pages // pp
    pt = page_table.reshape(-1)

    def pool_spec(shape, n):
        return pl.BlockSpec((1, 1) + shape,
                            lambda b, c, pt_ref: (layer, pt_ref[b * n_pages + c * pp + n], 0, 0))

    per_seq = lambda shape: pl.BlockSpec((1,) + shape, lambda b, c, pt_ref: (b, 0, 0))
    in_specs = ([per_seq((rows, w))]
                + [pool_spec((w, page), n) for n in range(pp)] + [pool_spec((w, page), n) for n in range(pp)]
                + [pool_spec((H_C, page), n) for n in range(pp)]
                + [per_seq((w, page)), per_seq((w, page)), per_seq((H_C, page)),
                   pl.BlockSpec(tri.shape, lambda b, c, pt_ref: (0, 0))])
    return pl.pallas_call(
        functools.partial(_fox_dec_kernel, pp=pp, t_new=t_new),
        grid_spec=pltpu.PrefetchScalarGridSpec(
            num_scalar_prefetch=1, grid=(bs, nch), in_specs=in_specs,
            out_specs=per_seq((t_new, w)),
            scratch_shapes=[pltpu.VMEM((rows, 1), F32), pltpu.VMEM((rows, 1), F32),
                            pltpu.VMEM((rows, w), F32), pltpu.VMEM((H_C, LANES), F32)]),
        out_shape=jax.ShapeDtypeStruct((bs, t_new, w), F32),
        compiler_params=_params("parallel", "arbitrary"),
        name="fox_decode",
    )(pt, q_bd, *([pool_k_t] * pp), *([pool_v_t] * pp), *([pool_lf_t] * pp), k_new_t, v_new_t, lf_new_t, tri)


def _rope_tables(pos):
    pos = pos.astype(F32)[:, None]

    def cs(half):
        inv = ROPE_THETA ** (-jnp.arange(half, dtype=F32) / half)
        ang = pos * inv[None, :]
        return jnp.cos(ang), jnp.sin(ang)

    c, s = cs(DK_A // 2)
    rope_a = jnp.concatenate([c, c, -s, s], axis=1)
    c, s = cs(ROPE_B // 2)
    z = lambda w: jnp.zeros((pos.shape[0], w), F32)
    rope_b = jnp.concatenate([c, c, z(LANES - ROPE_B), -s, z(LANES - ROPE_B // 2),
                              z(ROPE_B // 2), s, z(LANES - ROPE_B)], axis=1)
    return rope_a, rope_b


def _pad_cols(w, width):
    return jnp.pad(w, ((0, 0), (0, width - w.shape[1])))


def kernel(x_prompt, x_sample, state_ret, cache_mla_ckv, cache_mla_kpe, cache_fox_k, cache_fox_v,
           cache_fox_logf, page_table, p_prompt, p_sample, norm_gains, ffn_w_in, ffn_w_out, ple_w_gate,
           ple_w_proj, w_in0, ret_gn, mla_q_norm, mla_w_uq, mla_kv_norm, mla_w_uk, mla_w_uv, w_out0,
           w_in1, fox_b_f, w_out1):
    nb, seq, d = x_prompt.shape
    bs, t_new, _ = x_sample.shape
    depth = norm_gains.shape[0]
    n_p, n_s = nb * seq, bs * t_new
    n_pages, page = page_table.shape[1], cache_mla_ckv.shape[2]
    past_len = n_pages * page
    assert seq % RET_CHUNK == 0 and t_new <= RET_CHUNK

    x = jnp.concatenate([x_prompt.reshape(n_p, d), x_sample.reshape(n_s, d)], axis=0)
    pos = jnp.concatenate([jnp.tile(jnp.arange(seq, dtype=jnp.int32), nb),
                           jnp.tile(past_len + jnp.arange(t_new, dtype=jnp.int32), bs)])
    rope_a, rope_b = _rope_tables(pos)
    eye_b = jnp.eye(H_B, dtype=F32)
    ii = jnp.arange(LANES)
    tri = (ii[:, None] <= ii[None, :]).astype(BF16)

    rets_p, rets_s, ckvs, kpes, fks, fvs, fls = [], [], [], [], [], [], []
    for i in range(depth):
        g = norm_gains[i]
        j = i // 2
        x = _ffn(x, g, ffn_w_in[i, 0].astype(BF16), ffn_w_out[i, 0].astype(BF16), 0)
        if i % 2 == 0:
            w_in = _pad_cols(w_in0[j], IN0_PAD).astype(BF16)
            uq = mla_w_uq[j]
            w_uq = jnp.concatenate(
                [uq[:, :, :NOPE_B].reshape(Q_RANK, H_B * NOPE_B),
                 jnp.pad(uq[:, :, NOPE_B:], ((0, 0), (0, 0), (0, LANES - ROPE_B))).reshape(Q_RANK, H_B * LANES)],
                axis=1).astype(BF16)
            w_bd = jnp.einsum('hdr,hg->hdgr', jnp.transpose(mla_w_uk[j], (1, 2, 0)), eye_b)
            w_bd = w_bd.reshape(H_B * NOPE_B, H_B * KV_RANK).astype(BF16)
            w_uv = jnp.einsum('rhe,hg->hrge', mla_w_uv[j], eye_b).reshape(H_B, KV_RANK, H_B * VD_B).astype(BF16)
            ret, qc, kc, ckv, kpe = _even_in(x, g, w_in, rope_a, rope_b, mla_q_norm[j][None], mla_kv_norm[j][None],
                                             w_uq, w_bd)
            wq = 3 * H_A * DK_A
            o_rp, s_p = _retention(ret, jnp.zeros((nb, H_A, DK_A, DV_A), F32), nb, seq // RET_CHUNK, float(RET_CHUNK))
            ret_s = jnp.pad(ret[n_p:, :wq].reshape(bs, t_new, wq), ((0, 0), (0, RET_CHUNK - t_new), (0, 0)))
            o_rs, s_s = _retention(ret_s.reshape(bs * RET_CHUNK, wq), state_ret[j], bs, 1, float(t_new))
            o_rs = o_rs.reshape(bs, RET_CHUNK, H_A * DV_A)[:, :t_new].reshape(n_s, H_A * DV_A)
            o_r = jnp.concatenate([o_rp, o_rs], axis=0)
            ol_p = _mla_prompt(qc, kc, nb, seq)
            kw = kc.shape[1]
            q_s = qc[:, n_p:].reshape(H_B, bs, t_new, kw).transpose(1, 0, 2, 3).reshape(bs, H_B * t_new, kw)
            k_new = jnp.pad(kc[n_p:].reshape(bs, t_new, kw), ((0, 0), (0, page - t_new), (0, 0)))
            ol_s = _mla_decode(page_table, q_s, cache_mla_ckv, cache_mla_kpe.transpose(0, 1, 3, 2), k_new, j, t_new)
            ol_s = ol_s.reshape(bs, H_B, t_new, KV_RANK).transpose(1, 0, 2, 3).reshape(H_B, n_s, KV_RANK)
            o_lat = jnp.concatenate([ol_p, ol_s.astype(BF16)], axis=1)
            x = _even_out(x, o_r, ret, o_lat, g, ret_gn[j].reshape(1, H_A * DV_A), w_uv, w_out0[j].astype(BF16))
            rets_p.append(s_p)
            rets_s.append(s_s)
            ckvs.append(ckv)
            kpes.append(kpe)
        else:
            w = H_C * HD_C
            w_in = _pad_cols(w_in1[j], IN1_PAD).astype(BF16)
            b_f = jnp.pad(fox_b_f[j], (0, LANES - H_C))[None]
            qkv, fk, fv, fl = _odd_in(x, g, w_in, b_f)
            c_t = _cumsum_seq(fl[:n_p].reshape(nb, seq, H_C).transpose(0, 2, 1), tri)
            o_p = _fox_prompt(qkv, c_t.reshape(nb * H_C // 2, 2, seq), nb, seq)
            q4 = qkv[n_p:, :w].reshape(bs, t_new, H_C, 1, HD_C)
            q_bd = (q4 * jnp.eye(H_C, dtype=BF16)[None, None, :, :, None]).reshape(bs, t_new * H_C, w)
            keys_last = lambda a, c: jnp.pad(a.reshape(bs, t_new, c).transpose(0, 2, 1),
                                             ((0, 0), (0, 0), (0, page - t_new)))
            n_pool = cache_fox_k.shape[1]
            pool_t = lambda a: a.transpose(0, 1, 3, 4, 2).reshape(-1, n_pool, w, page)
            o_s = _fox_decode(page_table, q_bd, pool_t(cache_fox_k), pool_t(cache_fox_v),
                              cache_fox_logf.transpose(0, 1, 3, 2), keys_last(qkv[n_p:, w:2 * w], w),
                              keys_last(qkv[n_p:, 2 * w:], w), keys_last(fl[n_p:], H_C), tri, j, t_new)
            a = jnp.concatenate([o_p, o_s.reshape(n_s, w).astype(BF16)], axis=0)
            x = _odd_out(x, a, g, w_out1[j].astype(BF16))
            fks.append(fk)
            fvs.append(fv)
            fls.append(fl)
        x = _ffn(x, g, ffn_w_in[i, 1].astype(BF16), ffn_w_out[i, 1].astype(BF16), 4)
        p = jnp.concatenate([p_prompt[i].reshape(n_p, -1), p_sample[i].reshape(n_s, -1)], axis=0)
        x = _ple(x, p, g, ple_w_gate[i].astype(BF16), ple_w_proj[i].astype(BF16))

    def split(parts, tail):
        a = jnp.stack(parts)
        return (a[:, :n_p].reshape((len(parts), nb, seq) + tail),
                a[:, n_p:].reshape((len(parts), bs, t_new) + tail))

    ckv_p, ckv_s = split(ckvs, (KV_RANK,))
    kpe_p, kpe_s = split(kpes, (ROPE_B,))
    fk_p, fk_s = split(fks, (H_C, HD_C))
    fv_p, fv_s = split(fvs, (H_C, HD_C))
    fl_p, fl_s = split(fls, (H_C,))
    return (x[:n_p].reshape(nb, seq, d), x[n_p:].reshape(bs, t_new, d), jnp.stack(rets_p), jnp.stack(rets_s),
            ckv_p, ckv_s, kpe_p, kpe_s, fk_p, fk_s, fv_p, fv_s, fl_p, fl_s)
```

```python
import functools

import jax
import jax.numpy as jnp
import numpy as np
from jax import lax
from jax.experimental import pallas as pl
from jax.experimental.pallas import tpu as pltpu

F32 = jnp.float32
BF16 = jnp.bfloat16

H_A, DK_A, DV_A = 4, 128, 128
RET_CHUNK = 128
H_B, NOPE_B, ROPE_B, VD_B = 8, 64, 32, 64
Q_RANK, KV_RANK = 256, 128
H_C, HD_C = 16, 64
ROPE_THETA = 10000.0
NORM_EPS = 1e-6
NEG_INF = -1e30
MLA_SCALE = (NOPE_B + ROPE_B) ** -0.5
FOX_SCALE = HD_C ** -0.5
LOG2E = 1.4426950408889634

LANES = 128
IN0_PAD = 2560
IN1_PAD = 3200
VMEM_LIMIT = 56 * 1024 * 1024

_NT = (((1,), (1,)), ((), ()))
_TN = (((0,), (0,)), ((), ()))


def _dot(a, b):
    return jnp.dot(a, b, preferred_element_type=F32)


def _dot_nt(a, b):
    return lax.dot_general(a, b, _NT, preferred_element_type=F32)


def _dot_tn(a, b):
    return lax.dot_general(a, b, _TN, preferred_element_type=F32)


def _rms(x, g):
    return x * lax.rsqrt(jnp.mean(x * x, axis=-1, keepdims=True) + NORM_EPS) * g


def _row_tile(n, cap=512):
    t = cap
    while n % t:
        t //= 2
    assert t >= 8, n
    return t


def _params(*sem):
    return pltpu.CompilerParams(dimension_semantics=sem, vmem_limit_bytes=VMEM_LIMIT)


def _resident(shape):
    nd = len(shape)
    return pl.BlockSpec(shape, lambda *_: (0,) * nd, pipeline_mode=pl.Buffered(1))


def _ffn_body(x, g_pre, g_post, win_ref, wout_ref, dff, chunk):
    h = _rms(x, g_pre).astype(BF16)
    acc = jnp.zeros(x.shape, F32)
    for c0 in range(0, dff, chunk):
        a = _dot(h, win_ref[:, c0:c0 + chunk])
        b = _dot(h, win_ref[:, dff + c0:dff + c0 + chunk])
        act = (a * jax.nn.sigmoid(a)) * b
        acc = acc + _dot(act.astype(BF16), wout_ref[c0:c0 + chunk, :])
    return x + 0.5 * _rms(acc, g_post)


def _ffn_kernel(x_ref, g_ref, win_ref, wout_ref, o_ref, *, dff, chunk, gi):
    o_ref[...] = _ffn_body(x_ref[...], g_ref[gi:gi + 1], g_ref[gi + 1:gi + 2], win_ref, wout_ref, dff, chunk)


def _ffn(x, gains, w_in, w_out, gi):
    n, d = x.shape
    dff = w_out.shape[0]
    chunk = 256 if dff % 256 == 0 else LANES
    tm = _row_tile(n)
    return pl.pallas_call(
        functools.partial(_ffn_kernel, dff=dff, chunk=chunk, gi=gi),
        grid=(n // tm,),
        in_specs=[pl.BlockSpec((tm, d), lambda i: (i, 0)), _resident(gains.shape),
                  _resident(w_in.shape), _resident(w_out.shape)],
        out_specs=pl.BlockSpec((tm, d), lambda i: (i, 0)),
        out_shape=jax.ShapeDtypeStruct((n, d), F32),
        compiler_params=_params("parallel"),
        name="ffn",
    )(x, gains, w_in, w_out)


def _ple_kernel(x_ref, p_ref, g_ref, wg_ref, wp_ref, o_ref):
    x = x_ref[...]
    gate = jax.nn.sigmoid(_dot(_rms(x, g_ref[6:7]).astype(BF16), wg_ref[...]))
    e = _dot(p_ref[...].astype(BF16), wp_ref[...])
    o_ref[...] = x + _rms(gate * e, g_ref[7:8])


def _ple(x, p, gains, w_gate, w_proj):
    n, d = x.shape
    tm = _row_tile(n)
    return pl.pallas_call(
        _ple_kernel,
        grid=(n // tm,),
        in_specs=[pl.BlockSpec((tm, d), lambda i: (i, 0)), pl.BlockSpec((tm, p.shape[1]), lambda i: (i, 0)),
                  _resident(gains.shape), _resident(w_gate.shape), _resident(w_proj.shape)],
        out_specs=pl.BlockSpec((tm, d), lambda i: (i, 0)),
        out_shape=jax.ShapeDtypeStruct((n, d), F32),
        compiler_params=_params("parallel"),
        name="ple",
    )(x, p, gains, w_gate, w_proj)


def _even_in_kernel(x_ref, g_ref, win_ref, ra_ref, rb_ref, qn_ref, kvn_ref, wuq_ref, wbd_ref,
                    ret_ref, qc_ref, kc_ref, ckv_ref, kpe_ref):
    h = _rms(x_ref[...], g_ref[2:3]).astype(BF16)
    z = _dot(h, win_ref[...])
    hw = H_A * DK_A
    cosr, sinr = ra_ref[:, 0:LANES], ra_ref[:, LANES:2 * LANES]
    for hh in range(H_A):
        for base, sc in ((0, None), (hw, DK_A ** -0.5)):
            xh = z[:, base + hh * DK_A:base + (hh + 1) * DK_A]
            r = xh * cosr + pltpu.roll(xh, DK_A // 2, 1) * sinr
            ret_ref[:, base + hh * DK_A:base + (hh + 1) * DK_A] = r if sc is None else r * sc
    ret_ref[:, 2 * hw:4 * hw] = z[:, 2 * hw:4 * hw]

    c_cos, s_lo, s_hi = rb_ref[:, 0:LANES], rb_ref[:, LANES:2 * LANES], rb_ref[:, 2 * LANES:3 * LANES]

    def rope32(v):
        return (v * c_cos + pltpu.roll(v, LANES - ROPE_B // 2, 1) * s_lo
                + pltpu.roll(v, ROPE_B // 2, 1) * s_hi)

    o = 4 * hw
    cq = _rms(z[:, o:o + Q_RANK], qn_ref[...]).astype(BF16)
    q = _dot(cq, wuq_ref[...])
    nq = H_B * NOPE_B
    qlat = _dot(q[:, :nq].astype(BF16), wbd_ref[...])
    ckv = _rms(z[:, o + Q_RANK:o + Q_RANK + KV_RANK], kvn_ref[...])
    kpe = rope32(z[:, o + Q_RANK + KV_RANK:o + Q_RANK + KV_RANK + LANES])
    ckv_ref[...] = ckv
    kpe_ref[...] = kpe[:, :ROPE_B]
    kc_ref[:, 0:KV_RANK] = ckv.astype(BF16)
    kc_ref[:, KV_RANK:KV_RANK + LANES] = kpe.astype(BF16)
    for hh in range(H_B):
        qc_ref[hh, :, 0:KV_RANK] = qlat[:, hh * KV_RANK:(hh + 1) * KV_RANK].astype(BF16)
        qc_ref[hh, :, KV_RANK:KV_RANK + LANES] = rope32(q[:, nq + hh * LANES:nq + (hh + 1) * LANES]).astype(BF16)


def _even_in(x, gains, w_in, rope_a, rope_b, qn, kvn, w_uq, w_bd):
    n, d = x.shape
    tm = _row_tile(n)
    kw = KV_RANK + LANES
    row = lambda w: pl.BlockSpec((tm, w), lambda i: (i, 0))
    return pl.pallas_call(
        _even_in_kernel,
        grid=(n // tm,),
        in_specs=[row(d), _resident(gains.shape), _resident(w_in.shape), row(rope_a.shape[1]),
                  row(rope_b.shape[1]), _resident(qn.shape), _resident(kvn.shape), _resident(w_uq.shape),
                  _resident(w_bd.shape)],
        out_specs=[row(4 * H_A * DK_A), pl.BlockSpec((H_B, tm, kw), lambda i: (0, i, 0)), row(kw),
                   row(KV_RANK), row(ROPE_B)],
        out_shape=[jax.ShapeDtypeStruct((n, 4 * H_A * DK_A), F32), jax.ShapeDtypeStruct((H_B, n, kw), BF16),
                   jax.ShapeDtypeStruct((n, kw), BF16), jax.ShapeDtypeStruct((n, KV_RANK), F32),
                   jax.ShapeDtypeStruct((n, ROPE_B), F32)],
        compiler_params=_params("parallel"),
        name="even_in",
    )(x, gains, w_in, rope_a, rope_b, qn, kvn, w_uq, w_bd)


def _even_out_kernel(x_ref, or_ref, gr_ref, ol_ref, g_ref, gn_ref, wuv_ref, wout_ref, o_ref):
    hw = H_A * DV_A
    gate = gr_ref[...]
    gate = gate * jax.nn.sigmoid(gate)
    y = None
    for hh in range(H_A):
        sl = slice(hh * DV_A, (hh + 1) * DV_A)
        oh = or_ref[:, sl]
        xc = oh - jnp.mean(oh, axis=-1, keepdims=True)
        var = jnp.mean(xc * xc, axis=-1, keepdims=True)
        a = gate[:, sl] * (xc * lax.rsqrt(var + NORM_EPS) * gn_ref[:, sl])
        t = _dot(a.astype(BF16), wout_ref[sl, :])
        y = t if y is None else y + t
    om = None
    for hh in range(H_B):
        t = _dot(ol_ref[hh], wuv_ref[hh])
        om = t if om is None else om + t
    y = y + _dot(om.astype(BF16), wout_ref[hw:, :])
    o_ref[...] = x_ref[...] + _rms(y, g_ref[3:4])


def _even_out(x, o_r, ret, o_lat, gains, gn, w_uv, w_out):
    n, d = x.shape
    tm = _row_tile(n)
    hw = H_A * DV_A
    return pl.pallas_call(
        _even_out_kernel,
        grid=(n // tm,),
        in_specs=[pl.BlockSpec((tm, d), lambda i: (i, 0)), pl.BlockSpec((tm, hw), lambda i: (i, 0)),
                  pl.BlockSpec((tm, hw), lambda i: (i, 3)),
                  pl.BlockSpec((H_B, tm, KV_RANK), lambda i: (0, i, 0)),
                  _resident(gains.shape), _resident(gn.shape), _resident(w_uv.shape), _resident(w_out.shape)],
        out_specs=pl.BlockSpec((tm, d), lambda i: (i, 0)),
        out_shape=jax.ShapeDtypeStruct((n, d), F32),
        compiler_params=_params("parallel"),
        name="even_out",
    )(x, o_r, ret, o_lat, gains, gn, w_uv, w_out)


def _odd_in_kernel(x_ref, g_ref, win_ref, bf_ref, qkv_ref, fk_ref, fv_ref, fl_ref):
    h = _rms(x_ref[...], g_ref[2:3]).astype(BF16)
    z = _dot(h, win_ref[...])
    w = H_C * HD_C
    qkv_ref[:, 0:w] = (z[:, 0:w] * FOX_SCALE).astype(BF16)
    qkv_ref[:, w:3 * w] = z[:, w:3 * w].astype(BF16)
    fk_ref[...] = z[:, w:2 * w]
    fv_ref[...] = z[:, 2 * w:3 * w]
    t = z[:, 3 * w:3 * w + LANES] + bf_ref[...]
    ls = jnp.minimum(t, 0.0) - jnp.log1p(jnp.exp(-jnp.abs(t)))
    fl_ref[...] = ls[:, :H_C]


def _odd_in(x, gains, w_in, b_f):
    n, d = x.shape
    tm = _row_tile(n)
    w = H_C * HD_C
    row = lambda c: pl.BlockSpec((tm, c), lambda i: (i, 0))
    return pl.pallas_call(
        _odd_in_kernel,
        grid=(n // tm,),
        in_specs=[row(d), _resident(gains.shape), _resident(w_in.shape), _resident(b_f.shape)],
        out_specs=[row(3 * w), row(w), row(w), row(H_C)],
        out_shape=[jax.ShapeDtypeStruct((n, 3 * w), BF16), jax.ShapeDtypeStruct((n, w), F32),
                   jax.ShapeDtypeStruct((n, w), F32), jax.ShapeDtypeStruct((n, H_C), F32)],
        compiler_params=_params("parallel"),
        name="odd_in",
    )(x, gains, w_in, b_f)


def _odd_out_kernel(x_ref, a_ref, g_ref, wout_ref, o_ref):
    o_ref[...] = x_ref[...] + _rms(_dot(a_ref[...], wout_ref[...]), g_ref[3:4])


def _odd_out(x, a, gains, w_out):
    n, d = x.shape
    tm = _row_tile(n)
    return pl.pallas_call(
        _odd_out_kernel,
        grid=(n // tm,),
        in_specs=[pl.BlockSpec((tm, d), lambda i: (i, 0)), pl.BlockSpec((tm, a.shape[1]), lambda i: (i, 0)),
                  _resident(gains.shape), _resident(w_out.shape)],
        out_specs=pl.BlockSpec((tm, d), lambda i: (i, 0)),
        out_shape=jax.ShapeDtypeStruct((n, d), F32),
        compiler_params=_params("parallel"),
        name="odd_out",
    )(x, a, gains, w_out)


def _ret_kernel(qkv_ref, s0_ref, dec_ref, qd_ref, kd_ref, sd_ref, o_ref, s_ref):
    @pl.when(pl.program_id(1) == 0)
    def _():
        s_ref[...] = s0_ref[...]

    for hh in range(H_A):
        q = qkv_ref[:, hh * DK_A:(hh + 1) * DK_A]
        k = qkv_ref[:, (H_A + hh) * DK_A:(H_A + hh + 1) * DK_A]
        v = qkv_ref[:, 2 * H_A * DK_A + hh * DV_A:2 * H_A * DK_A + (hh + 1) * DV_A].astype(BF16)
        s = s_ref[0, hh]
        qb = q.astype(BF16)
        scores = _dot_nt(qb, k.astype(BF16)) * dec_ref[hh]
        o = _dot(scores.astype(BF16), v) + _dot(qb, s.astype(BF16)) * qd_ref[hh]
        o_ref[:, hh * DV_A:(hh + 1) * DV_A] = o
        s_ref[0, hh] = sd_ref[hh] * s + _dot_tn((k * kd_ref[hh]).astype(BF16), v)


def _retention(qkv, s0, nb, nc, length):
    lp = RET_CHUNK
    log_gamma = jnp.log1p(-jnp.exp2(-5.0 - jnp.arange(H_A, dtype=F32)))
    idx = jnp.arange(lp, dtype=F32)
    diff = idx[:, None] - idx[None, :]
    decay = jnp.where(diff >= 0, jnp.exp(log_gamma[:, None, None] * jnp.maximum(diff, 0.0)), 0.0)
    ones = jnp.ones((1, 1, DV_A), F32)
    qd = jnp.exp(log_gamma[:, None] * (idx[None, :] + 1.0))[:, :, None] * ones
    kd = jnp.exp(log_gamma[:, None] * (length - 1.0 - idx[None, :]))[:, :, None] * ones
    sd = jnp.exp(log_gamma * length)[:, None, None] * ones
    wq = 3 * H_A * DK_A
    return pl.pallas_call(
        _ret_kernel,
        grid=(nb, nc),
        in_specs=[pl.BlockSpec((lp, wq), lambda b, c: (b * nc + c, 0)),
                  pl.BlockSpec((1, H_A, DK_A, DV_A), lambda b, c: (b, 0, 0, 0)),
                  _resident(decay.shape), _resident(qd.shape), _resident(kd.shape), _resident(sd.shape)],
        out_specs=[pl.BlockSpec((lp, H_A * DV_A), lambda b, c: (b * nc + c, 0)),
                   pl.BlockSpec((1, H_A, DK_A, DV_A), lambda b, c: (b, 0, 0, 0))],
        out_shape=[jax.ShapeDtypeStruct((nb * nc * lp, H_A * DV_A), F32),
                   jax.ShapeDtypeStruct((nb, H_A, DK_A, DV_A), F32)],
        compiler_params=_params("parallel", "arbitrary"),
        name="retention",
    )(qkv, s0, decay, qd, kd, sd)


def _online_update(m_ref, l_ref, acc_ref, s, pv):
    m_prev = m_ref[...]
    m_new = jnp.maximum(m_prev, jnp.max(s, axis=1, keepdims=True))
    alpha = jnp.exp(m_prev - m_new)
    p = jnp.exp(s - m_new)
    l_ref[...] = alpha * l_ref[...] + jnp.sum(p, axis=1, keepdims=True)
    acc_ref[...] = alpha * acc_ref[...] + pv(p.astype(BF16))
    m_ref[...] = m_new


def _init_state(m_ref, l_ref, acc_ref):
    m_ref[...] = jnp.full(m_ref.shape, NEG_INF, F32)
    l_ref[...] = jnp.zeros(l_ref.shape, F32)
    acc_ref[...] = jnp.zeros(acc_ref.shape, F32)


def _mla_flash_kernel(q_ref, k_ref, o_ref, m_ref, acc_ref, s_ref, *, tq, tk):
    i = pl.program_id(1)
    rows = H_B * tq
    q = q_ref[...].reshape(rows, q_ref.shape[2])
    m_ref[...] = jnp.full(m_ref.shape, NEG_INF, F32)
    acc_ref[...] = jnp.zeros(acc_ref.shape, F32)
    ones = jnp.ones((tk, LANES), BF16)
    c = MLA_SCALE * LOG2E

    def scores(slot, j):
        s_ref[slot] = _dot_nt(q, k_ref[pl.ds(pl.multiple_of(j * tk, tk), tk), :])

    def update(slot, j, masked):
        off = pl.multiple_of(j * tk, tk)
        s = s_ref[slot]
        if masked:
            qpos = i * tq + (lax.broadcasted_iota(jnp.int32, (rows, tk), 0) & (tq - 1))
            kpos = off + lax.broadcasted_iota(jnp.int32, (rows, tk), 1)
            s = jnp.where(kpos <= qpos, s, NEG_INF)
        m_prev = m_ref[...]
        m_new = jnp.maximum(m_prev, jnp.max(s, axis=1, keepdims=True))
        p = jnp.exp2(((s - m_new) * c).astype(BF16))
        v = jnp.concatenate([k_ref[pl.ds(off, tk), 0:KV_RANK], ones], axis=1)
        acc_ref[...] = jnp.exp2((m_prev - m_new) * c) * acc_ref[...] + _dot(p, v)
        m_ref[...] = m_new

    nfull = (i * tq) // tk
    npair = nfull // 2
    scores(0, 0)

    def body(jj, carry):
        j = 2 * jj
        scores(1, j + 1)
        update(0, j, False)
        scores(0, j + 2)
        update(1, j + 1, False)
        return carry

    lax.fori_loop(0, npair, body, 0)

    @pl.when(nfull % 2 == 1)
    def _():
        scores(1, nfull)
        update(0, nfull - 1, False)
        update(1, nfull, True)

    @pl.when(nfull % 2 == 0)
    def _():
        update(0, nfull, True)

    o = acc_ref[:, 0:KV_RANK] / acc_ref[:, KV_RANK:KV_RANK + LANES]
    o_ref[...] = o.reshape(H_B, tq, KV_RANK).astype(BF16)


def _mla_prompt(qc, kc, nb, seq):
    tq = min(128, seq)
    tk = min(512, seq)
    nq = seq // tq
    kw = kc.shape[1]
    return pl.pallas_call(
        functools.partial(_mla_flash_kernel, tq=tq, tk=tk),
        grid=(nb, nq),
        in_specs=[pl.BlockSpec((H_B, tq, kw), lambda b, i: (0, b * nq + i, 0)),
                  pl.BlockSpec((seq, kw), lambda b, i: (b, 0))],
        out_specs=pl.BlockSpec((H_B, tq, KV_RANK), lambda b, i: (0, b * nq + i, 0)),
        out_shape=jax.ShapeDtypeStruct((H_B, nb * seq, KV_RANK), BF16),
        scratch_shapes=[pltpu.VMEM((H_B * tq, 1), F32), pltpu.VMEM((H_B * tq, KV_RANK + LANES), F32),
                        pltpu.VMEM((2, H_B * tq, tk), F32)],
        compiler_params=_params("parallel", "arbitrary"),
        name="mla_prompt",
    )(qc, kc)


def _cumsum_lanes(x, tri):
    hi = x.astype(BF16)
    r = x - hi.astype(F32)
    mid = r.astype(BF16)
    lo = (r - mid.astype(F32)).astype(BF16)
    cc = _dot(jnp.concatenate([hi, mid, lo], axis=0), tri)
    n = x.shape[0]
    return cc[0:n] + cc[n:2 * n] + cc[2 * n:3 * n]


def _cumsum_kernel(x_ref, tri_ref, o_ref, c_ref):
    @pl.when(pl.program_id(1) == 0)
    def _():
        c_ref[...] = jnp.zeros(c_ref.shape, F32)

    c = _cumsum_lanes(x_ref[0], tri_ref[...]) + c_ref[...]
    o_ref[0] = c
    c_ref[...] = jnp.broadcast_to(c[:, LANES - 1:LANES], c_ref.shape)


def _cumsum_seq(x_t, tri):
    nb, h, seq = x_t.shape
    return pl.pallas_call(
        _cumsum_kernel,
        grid=(nb, seq // LANES),
        in_specs=[pl.BlockSpec((1, h, LANES), lambda b, j: (b, 0, j)), _resident(tri.shape)],
        out_specs=pl.BlockSpec((1, h, LANES), lambda b, j: (b, 0, j)),
        out_shape=jax.ShapeDtypeStruct((nb, h, seq), F32),
        scratch_shapes=[pltpu.VMEM((h, LANES), F32)],
        compiler_params=_params("parallel", "arbitrary"),
        name="cumsum",
    )(x_t, tri)


def _fox_flash_kernel(q_ref, k_ref, v_ref, c_ref, o_ref, ma, acca, mb, accb, s_ref, *, tq, tk):
    i = pl.program_id(2)
    q2 = q_ref[...]
    lane = lax.broadcasted_iota(jnp.int32, q2.shape, 1)
    zero = jnp.zeros_like(q2)
    qa = jnp.where(lane < HD_C, q2, zero)
    qb = jnp.where(lane >= HD_C, q2, zero)
    for m_ref, acc_ref in ((ma, acca), (mb, accb)):
        m_ref[...] = jnp.full(m_ref.shape, NEG_INF, F32)
        acc_ref[...] = jnp.zeros(acc_ref.shape, F32)
    lane_v = lax.broadcasted_iota(jnp.int32, (tk, LANES), 1)
    one = jnp.ones((tk, LANES), BF16)

    def scores(slot, j):
        off = pl.multiple_of(j * tk, tk)
        k = k_ref[pl.ds(off, tk), :]
        cb = c_ref[0, :, pl.ds(off, tk)]
        s_ref[slot, 0] = _dot_nt(qa, k) - cb[0:1, :]
        s_ref[slot, 1] = _dot_nt(qb, k) - cb[1:2, :]

    def update(slot, j, masked):
        off = pl.multiple_of(j * tk, tk)
        v = v_ref[pl.ds(off, tk), :]
        vs = (jnp.where(lane_v < HD_C, v, one), jnp.where(lane_v >= HD_C, v, one))
        if masked:
            qpos = i * tq + lax.broadcasted_iota(jnp.int32, (tq, tk), 0)
            kpos = off + lax.broadcasted_iota(jnp.int32, (tq, tk), 1)
            keep = kpos <= qpos
        for h, (vx, m_ref, acc_ref) in enumerate(zip(vs, (ma, mb), (acca, accb))):
            s = s_ref[slot, h]
            if masked:
                s = jnp.where(keep, s, NEG_INF)
            m_prev = m_ref[...]
            m_new = jnp.maximum(m_prev, jnp.max(s, axis=1, keepdims=True))
            p = jnp.exp((s - m_new).astype(BF16))
            acc_ref[...] = jnp.exp(m_prev - m_new) * acc_ref[...] + _dot(p, vx)
            m_ref[...] = m_new

    nfull = (i * tq) // tk
    npair = nfull // 2
    scores(0, 0)

    def body(jj, carry):
        j = 2 * jj
        scores(1, j + 1)
        update(0, j, False)
        scores(0, j + 2)
        update(1, j + 1, False)
        return carry

    lax.fori_loop(0, npair, body, 0)

    @pl.when(nfull % 2 == 1)
    def _():
        scores(1, nfull)
        update(0, nfull - 1, False)
        update(1, nfull, True)

    @pl.when(nfull % 2 == 0)
    def _():
        update(0, nfull, True)

    a, b = acca[...], accb[...]
    lane_o = lax.broadcasted_iota(jnp.int32, (tq, LANES), 1)
    o = jnp.where(lane_o < HD_C, a / pltpu.roll(a, HD_C, 1), b / pltpu.roll(b, HD_C, 1))
    o_ref[...] = o.astype(BF16)


def _fox_prompt(qkv, c_pairs, nb, seq):
    tq = min(512, seq)
    tk = min(512, seq)
    nq = seq // tq
    npair = H_C * HD_C // LANES
    st = lambda w: pltpu.VMEM((tq, w), F32)
    return pl.pallas_call(
        functools.partial(_fox_flash_kernel, tq=tq, tk=tk),
        grid=(nb, npair, nq),
        in_specs=[pl.BlockSpec((tq, LANES), lambda b, h, i: (b * nq + i, h)),
                  pl.BlockSpec((seq, LANES), lambda b, h, i: (b, npair + h)),
                  pl.BlockSpec((seq, LANES), lambda b, h, i: (b, 2 * npair + h)),
                  pl.BlockSpec((1, 2, seq), lambda b, h, i: (b * npair + h, 0, 0))],
        out_specs=pl.BlockSpec((tq, LANES), lambda b, h, i: (b * nq + i, h)),
        out_shape=jax.ShapeDtypeStruct((nb * seq, H_C * HD_C), BF16),
        scratch_shapes=[st(1), st(LANES), st(1), st(LANES), pltpu.VMEM((2, 2, tq, tk), F32)],
        compiler_params=_params("parallel", "parallel", "arbitrary"),
        name="fox_prompt",
    )(qkv, qkv, qkv, c_pairs)


def _mla_dec_kernel(pt_ref, q_ref, *refs, pp, t_new):
    del pt_ref
    ck_refs, kp_refs = refs[:pp], refs[pp:2 * pp]
    kn_ref, o_ref, m_ref, l_ref, acc_ref = refs[2 * pp:]
    c = pl.program_id(1)

    @pl.when(c == 0)
    def _():
        _init_state(m_ref, l_ref, acc_ref)

    q = q_ref[0]
    ql, qp = q[:, :KV_RANK], q[:, KV_RANK:KV_RANK + ROPE_B]
    cks = [r[0, 0].astype(BF16) for r in ck_refs]
    s = jnp.concatenate([_dot_nt(ql, ck) + _dot(qp, kp[0, 0].astype(BF16))
                         for ck, kp in zip(cks, kp_refs)], axis=1) * MLA_SCALE

    def pv(p):
        out = None
        for n, ck in enumerate(cks):
            t = _dot(p[:, n * LANES:(n + 1) * LANES], ck)
            out = t if out is None else out + t
        return out

    _online_update(m_ref, l_ref, acc_ref, s, pv)

    @pl.when(c == pl.num_programs(1) - 1)
    def _():
        kn = kn_ref[0]
        sn = _dot_nt(q, kn) * MLA_SCALE
        tok = lax.broadcasted_iota(jnp.int32, sn.shape, 0) & (t_new - 1)
        col = lax.broadcasted_iota(jnp.int32, sn.shape, 1)
        sn = jnp.where(col <= tok, sn, NEG_INF)
        _online_update(m_ref, l_ref, acc_ref, sn, lambda p: _dot(p, kn[:, :KV_RANK]))
        o_ref[0] = acc_ref[...] / l_ref[...]


def _mla_decode(page_table, q, pool_ckv, pool_kpe, k_new, layer, t_new):
    bs, rows, kw = q.shape
    n_pages = page_table.shape[1]
    page = pool_ckv.shape[2]
    pp = min(16, n_pages)
    assert page == LANES and n_pages % pp == 0 and t_new & (t_new - 1) == 0
    nch = n_pages // pp
    pt = page_table.reshape(-1)

    def pool_spec(shape, n):
        return pl.BlockSpec((1, 1) + shape,
                            lambda b, c, pt_ref: (layer, pt_ref[b * n_pages + c * pp + n], 0, 0))

    in_specs = ([pl.BlockSpec((1, rows, kw), lambda b, c, pt_ref: (b, 0, 0))]
                + [pool_spec((page, KV_RANK), n) for n in range(pp)]
                + [pool_spec((ROPE_B, page), n) for n in range(pp)]
                + [pl.BlockSpec((1, page, kw), lambda b, c, pt_ref: (b, 0, 0))])
    return pl.pallas_call(
        functools.partial(_mla_dec_kernel, pp=pp, t_new=t_new),
        grid_spec=pltpu.PrefetchScalarGridSpec(
            num_scalar_prefetch=1, grid=(bs, nch), in_specs=in_specs,
            out_specs=pl.BlockSpec((1, rows, KV_RANK), lambda b, c, pt_ref: (b, 0, 0)),
            scratch_shapes=[pltpu.VMEM((rows, 1), F32), pltpu.VMEM((rows, 1), F32),
                            pltpu.VMEM((rows, KV_RANK), F32)]),
        out_shape=jax.ShapeDtypeStruct((bs, rows, KV_RANK), F32),
        compiler_params=_params("parallel", "arbitrary"),
        name="mla_decode",
    )(pt, q, *([pool_ckv] * pp), *([pool_kpe] * pp), k_new)


def _fox_dec_kernel(pt_ref, q_ref, *refs, pp, t_new):
    del pt_ref
    k_refs, v_refs, lf_refs = refs[:pp], refs[pp:2 * pp], refs[2 * pp:3 * pp]
    kn_ref, vn_ref, lfn_ref, tri_ref, o_ref, m_ref, l_ref, acc_ref, c_ref = refs[3 * pp:]
    c = pl.program_id(1)

    @pl.when(c == 0)
    def _():
        _init_state(m_ref, l_ref, acc_ref)
        c_ref[...] = jnp.zeros(c_ref.shape, F32)

    q = q_ref[0]
    tri = tri_ref[...]

    def scores(k_bf16, cpage):
        return _dot(q, k_bf16) - jnp.concatenate([cpage] * t_new, axis=0)

    def prefixes(lf_pages):
        local = _cumsum_lanes(jnp.concatenate(lf_pages, axis=0), tri)
        carry = c_ref[...]
        out = []
        for n in range(len(lf_pages)):
            cpage = local[n * H_C:(n + 1) * H_C] + carry
            carry = jnp.broadcast_to(cpage[:, LANES - 1:LANES], carry.shape)
            out.append(cpage)
        c_ref[...] = carry
        return out

    def pv_pages(vs):
        def pv(p):
            out = None
            for n, v in enumerate(vs):
                t = _dot_nt(p[:, n * LANES:(n + 1) * LANES], v)
                out = t if out is None else out + t
            return out
        return pv

    cps = prefixes([r[0, 0] for r in lf_refs])
    s = jnp.concatenate([scores(kr[0, 0].astype(BF16), cp) for kr, cp in zip(k_refs, cps)], axis=1)
    _online_update(m_ref, l_ref, acc_ref, s, pv_pages([r[0, 0].astype(BF16) for r in v_refs]))

    @pl.when(c == pl.num_programs(1) - 1)
    def _():
        sn = scores(kn_ref[0], prefixes([lfn_ref[0]])[0])
        tok = lax.broadcasted_iota(jnp.int32, sn.shape, 0) // H_C
        col = lax.broadcasted_iota(jnp.int32, sn.shape, 1)
        sn = jnp.where(col <= tok, sn, NEG_INF)
        _online_update(m_ref, l_ref, acc_ref, sn, pv_pages([vn_ref[0]]))
        o = acc_ref[...] / l_ref[...]
        row_h = lax.broadcasted_iota(jnp.int32, o.shape, 0) % H_C
        lane_h = lax.broadcasted_iota(jnp.int32, o.shape, 1) // HD_C
        o = jnp.where(row_h == lane_h, o, 0.0)
        o_ref[0] = jnp.sum(o.reshape(t_new, H_C, o.shape[1]), axis=1)


def _fox_decode(page_table, q_bd, pool_k_t, pool_v_t, pool_lf_t, k_new_t, v_new_t, lf_new_t, tri, layer, t_new):
    bs, rows, w = q_bd.shape
    n_pages = page_table.shape[1]
    page = pool_k_t.shape[3]
    pp = min(8, n_pages)
    assert page == LANES and n_pages % pp == 0
    nch = n_pages // pp
    pt = page_table.reshape(-1)

    def pool_spec(shape, n):
        return pl.BlockSpec((1, 1) + shape,
                            lambda b, c, pt_ref: (layer, pt_ref[b * n_pages + c * pp + n], 0, 0))

    per_seq = lambda shape: pl.BlockSpec((1,) + shape, lambda b, c, pt_ref: (b, 0, 0))
    in_specs = ([per_seq((rows, w))]
                + [pool_spec((w, page), n) for n in range(pp)] + [pool_spec((w, page), n) for n in range(pp)]
                + [pool_spec((H_C, page), n) for n in range(pp)]
                + [per_seq((w, page)), per_seq((w, page)), per_seq((H_C, page)),
                   pl.BlockSpec(tri.shape, lambda b, c, pt_ref: (0, 0))])
    return pl.pallas_call(
        functools.partial(_fox_dec_kernel, pp=pp, t_new=t_new),
        grid_spec=pltpu.PrefetchScalarGridSpec(
            num_scalar_prefetch=1, grid=(bs, nch), in_specs=in_specs,
            out_specs=per_seq((t_new, w)),
            scratch_shapes=[pltpu.VMEM((rows, 1), F32), pltpu.VMEM((rows, 1), F32),
                            pltpu.VMEM((rows, w), F32), pltpu.VMEM((H_C, LANES), F32)]),
        out_shape=jax.ShapeDtypeStruct((bs, t_new, w), F32),
        compiler_params=_params("parallel", "arbitrary"),
        name="fox_decode",
    )(pt, q_bd, *([pool_k_t] * pp), *([pool_v_t] * pp), *([pool_lf_t] * pp), k_new_t, v_new_t, lf_new_t, tri)


def _rope_tables(pos):
    pos = pos.astype(F32)[:, None]

    def cs(half):
        inv = ROPE_THETA ** (-jnp.arange(half, dtype=F32) / half)
        ang = pos * inv[None, :]
        return jnp.cos(ang), jnp.sin(ang)

    c, s = cs(DK_A // 2)
    rope_a = jnp.concatenate([c, c, -s, s], axis=1)
    c, s = cs(ROPE_B // 2)
    z = lambda w: jnp.zeros((pos.shape[0], w), F32)
    rope_b = jnp.concatenate([c, c, z(LANES - ROPE_B), -s, z(LANES - ROPE_B // 2),
                              z(ROPE_B // 2), s, z(LANES - ROPE_B)], axis=1)
    return rope_a, rope_b


def _pad_cols(w, width):
    return jnp.pad(w, ((0, 0), (0, width - w.shape[1])))


def kernel(x_prompt, x_sample, state_ret, cache_mla_ckv, cache_mla_kpe, cache_fox_k, cache_fox_v,
           cache_fox_logf, page_table, p_prompt, p_sample, norm_gains, ffn_w_in, ffn_w_out, ple_w_gate,
           ple_w_proj, w_in0, ret_gn, mla_q_norm, mla_w_uq, mla_kv_norm, mla_w_uk, mla_w_uv, w_out0,
           w_in1, fox_b_f, w_out1):
    nb, seq, d = x_prompt.shape
    bs, t_new, _ = x_sample.shape
    depth = norm_gains.shape[0]
    n_p, n_s = nb * seq, bs * t_new
    n_pages, page = page_table.shape[1], cache_mla_ckv.shape[2]
    past_len = n_pages * page
    assert seq % RET_CHUNK == 0 and t_new <= RET_CHUNK

    x = jnp.concatenate([x_prompt.reshape(n_p, d), x_sample.reshape(n_s, d)], axis=0)
    pos = jnp.concatenate([jnp.tile(jnp.arange(seq, dtype=jnp.int32), nb),
                           jnp.tile(past_len + jnp.arange(t_new, dtype=jnp.int32), bs)])
    rope_a, rope_b = _rope_tables(pos)
    eye_b = jnp.eye(H_B, dtype=F32)
    ii = jnp.arange(LANES)
    tri = (ii[:, None] <= ii[None, :]).astype(BF16)

    rets_p, rets_s, ckvs, kpes, fks, fvs, fls = [], [], [], [], [], [], []
    for i in range(depth):
        g = norm_gains[i]
        j = i // 2
        x = _ffn(x, g, ffn_w_in[i, 0].astype(BF16), ffn_w_out[i, 0].astype(BF16), 0)
        if i % 2 == 0:
            w_in = _pad_cols(w_in0[j], IN0_PAD).astype(BF16)
            uq = mla_w_uq[j]
            w_uq = jnp.concatenate(
                [uq[:, :, :NOPE_B].reshape(Q_RANK, H_B * NOPE_B),
                 jnp.pad(uq[:, :, NOPE_B:], ((0, 0), (0, 0), (0, LANES - ROPE_B))).reshape(Q_RANK, H_B * LANES)],
                axis=1).astype(BF16)
            w_bd = jnp.einsum('hdr,hg->hdgr', jnp.transpose(mla_w_uk[j], (1, 2, 0)), eye_b)
            w_bd = w_bd.reshape(H_B * NOPE_B, H_B * KV_RANK).astype(BF16)
            w_uv = jnp.einsum('rhe,hg->hrge', mla_w_uv[j], eye_b).reshape(H_B, KV_RANK, H_B * VD_B).astype(BF16)
            ret, qc, kc, ckv, kpe = _even_in(x, g, w_in, rope_a, rope_b, mla_q_norm[j][None], mla_kv_norm[j][None],
                                             w_uq, w_bd)
            wq = 3 * H_A * DK_A
            o_rp, s_p = _retention(ret, jnp.zeros((nb, H_A, DK_A, DV_A), F32), nb, seq // RET_CHUNK, float(RET_CHUNK))
            ret_s = jnp.pad(ret[n_p:, :wq].reshape(bs, t_new, wq), ((0, 0), (0, RET_CHUNK - t_new), (0, 0)))
            o_rs, s_s = _retention(ret_s.reshape(bs * RET_CHUNK, wq), state_ret[j], bs, 1, float(t_new))
            o_rs = o_rs.reshape(bs, RET_CHUNK, H_A * DV_A)[:, :t_new].reshape(n_s, H_A * DV_A)
            o_r = jnp.concatenate([o_rp, o_rs], axis=0)
            ol_p = _mla_prompt(qc, kc, nb, seq)
            kw = kc.shape[1]
            q_s = qc[:, n_p:].reshape(H_B, bs, t_new, kw).transpose(1, 0, 2, 3).reshape(bs, H_B * t_new, kw)
            k_new = jnp.pad(kc[n_p:].reshape(bs, t_new, kw), ((0, 0), (0, page - t_new), (0, 0)))
            ol_s = _mla_decode(page_table, q_s, cache_mla_ckv, cache_mla_kpe.transpose(0, 1, 3, 2), k_new, j, t_new)
            ol_s = ol_s.reshape(bs, H_B, t_new, KV_RANK).transpose(1, 0, 2, 3).reshape(H_B, n_s, KV_RANK)
            o_lat = jnp.concatenate([ol_p, ol_s.astype(BF16)], axis=1)
            x = _even_out(x, o_r, ret, o_lat, g, ret_gn[j].reshape(1, H_A * DV_A), w_uv, w_out0[j].astype(BF16))
            rets_p.append(s_p)
            rets_s.append(s_s)
            ckvs.append(ckv)
            kpes.append(kpe)
        else:
            w = H_C * HD_C
            w_in = _pad_cols(w_in1[j], IN1_PAD).astype(BF16)
            b_f = jnp.pad(fox_b_f[j], (0, LANES - H_C))[None]
            qkv, fk, fv, fl = _odd_in(x, g, w_in, b_f)
            c_t = _cumsum_seq(fl[:n_p].reshape(nb, seq, H_C).transpose(0, 2, 1), tri)
            o_p = _fox_prompt(qkv, c_t.reshape(nb * H_C // 2, 2, seq), nb, seq)
            q4 = qkv[n_p:, :w].reshape(bs, t_new, H_C, 1, HD_C)
            q_bd = (q4 * jnp.eye(H_C, dtype=BF16)[None, None, :, :, None]).reshape(bs, t_new * H_C, w)
            keys_last = lambda a, c: jnp.pad(a.reshape(bs, t_new, c).transpose(0, 2, 1),
                                             ((0, 0), (0, 0), (0, page - t_new)))
            n_pool = cache_fox_k.shape[1]
            pool_t = lambda a: a.transpose(0, 1, 3, 4, 2).reshape(-1, n_pool, w, page)
            o_s = _fox_decode(page_table, q_bd, pool_t(cache_fox_k), pool_t(cache_fox_v),
                              cache_fox_logf.transpose(0, 1, 3, 2), keys_last(qkv[n_p:, w:2 * w], w),
                              keys_last(qkv[n_p:, 2 * w:], w), keys_last(fl[n_p:], H_C), tri, j, t_new)
            a = jnp.concatenate([o_p, o_s.reshape(n_s, w).astype(BF16)], axis=0)
            x = _odd_out(x, a, g, w_out1[j].astype(BF16))
            fks.append(fk)
            fvs.append(fv)
            fls.append(fl)
        x = _ffn(x, g, ffn_w_in[i, 1].astype(BF16), ffn_w_out[i, 1].astype(BF16), 4)
        p = jnp.concatenate([p_prompt[i].reshape(n_p, -1), p_sample[i].reshape(n_s, -1)], axis=0)
        x = _ple(x, p, g, ple_w_gate[i].astype(BF16), ple_w_proj[i].astype(BF16))

    def split(parts, tail):
        a = jnp.stack(parts)
        return (a[:, :n_p].reshape((len(parts), nb, seq) + tail),
                a[:, n_p:].reshape((len(parts), bs, t_new) + tail))

    ckv_p, ckv_s = split(ckvs, (KV_RANK,))
    kpe_p, kpe_s = split(kpes, (ROPE_B,))
    fk_p, fk_s = split(fks, (H_C, HD_C))
    fv_p, fv_s = split(fvs, (H_C, HD_C))
    fl_p, fl_s = split(fls, (H_C,))
    return (x[:n_p].reshape(nb, seq, d), x[n_p:].reshape(bs, t_new, d), jnp.stack(rets_p), jnp.stack(rets_s),
            ckv_p, ckv_s, kpe_p, kpe_s, fk_p, fk_s, fv_p, fv_s, fl_p, fl_s)
```

```python
import functools

import jax
import jax.numpy as jnp
import numpy as np
from jax import lax
from jax.experimental import pallas as pl
from jax.experimental.pallas import tpu as pltpu

F32 = jnp.float32
BF16 = jnp.bfloat16

H_A, DK_A, DV_A = 4, 128, 128
RET_CHUNK = 128
H_B, NOPE_B, ROPE_B, VD_B = 8, 64, 32, 64
Q_RANK, KV_RANK = 256, 128
H_C, HD_C = 16, 64
ROPE_THETA = 10000.0
NORM_EPS = 1e-6
NEG_INF = -1e30
MLA_SCALE = (NOPE_B + ROPE_B) ** -0.5
FOX_SCALE = HD_C ** -0.5
LOG2E = 1.4426950408889634

LANES = 128
IN0_PAD = 2560
IN1_PAD = 3200
VMEM_LIMIT = 56 * 1024 * 1024

_NT = (((1,), (1,)), ((), ()))
_TN = (((0,), (0,)), ((), ()))


def _dot(a, b):
    return jnp.dot(a, b, preferred_element_type=F32)


def _dot_nt(a, b):
    return lax.dot_general(a, b, _NT, preferred_element_type=F32)


def _dot_tn(a, b):
    return lax.dot_general(a, b, _TN, preferred_element_type=F32)


def _rms(x, g):
    return x * lax.rsqrt(jnp.mean(x * x, axis=-1, keepdims=True) + NORM_EPS) * g


def _row_tile(n, cap=512):
    t = cap
    while n % t:
        t //= 2
    assert t >= 8, n
    return t


def _params(*sem):
    return pltpu.CompilerParams(dimension_semantics=sem, vmem_limit_bytes=VMEM_LIMIT)


def _resident(shape):
    nd = len(shape)
    return pl.BlockSpec(shape, lambda *_: (0,) * nd, pipeline_mode=pl.Buffered(1))


def _ffn_body(x, g_pre, g_post, win_ref, wout_ref, dff, chunk):
    h = _rms(x, g_pre).astype(BF16)
    acc = jnp.zeros(x.shape, F32)
    for c0 in range(0, dff, chunk):
        a = _dot(h, win_ref[:, c0:c0 + chunk])
        b = _dot(h, win_ref[:, dff + c0:dff + c0 + chunk])
        act = (a * jax.nn.sigmoid(a)) * b
        acc = acc + _dot(act.astype(BF16), wout_ref[c0:c0 + chunk, :])
    return x + 0.5 * _rms(acc, g_post)


def _ffn_kernel(x_ref, g_ref, win_ref, wout_ref, o_ref, *, dff, chunk, gi):
    o_ref[...] = _ffn_body(x_ref[...], g_ref[gi:gi + 1], g_ref[gi + 1:gi + 2], win_ref, wout_ref, dff, chunk)


def _ffn(x, gains, w_in, w_out, gi):
    n, d = x.shape
    dff = w_out.shape[0]
    chunk = 256 if dff % 256 == 0 else LANES
    tm = _row_tile(n)
    return pl.pallas_call(
        functools.partial(_ffn_kernel, dff=dff, chunk=chunk, gi=gi),
        grid=(n // tm,),
        in_specs=[pl.BlockSpec((tm, d), lambda i: (i, 0)), _resident(gains.shape),
                  _resident(w_in.shape), _resident(w_out.shape)],
        out_specs=pl.BlockSpec((tm, d), lambda i: (i, 0)),
        out_shape=jax.ShapeDtypeStruct((n, d), F32),
        compiler_params=_params("parallel"),
        name="ffn",
    )(x, gains, w_in, w_out)


def _ple_kernel(x_ref, p_ref, g_ref, wg_ref, wp_ref, o_ref):
    x = x_ref[...]
    gate = jax.nn.sigmoid(_dot(_rms(x, g_ref[6:7]).astype(BF16), wg_ref[...]))
    e = _dot(p_ref[...].astype(BF16), wp_ref[...])
    o_ref[...] = x + _rms(gate * e, g_ref[7:8])


def _ple(x, p, gains, w_gate, w_proj):
    n, d = x.shape
    tm = _row_tile(n)
    return pl.pallas_call(
        _ple_kernel,
        grid=(n // tm,),
        in_specs=[pl.BlockSpec((tm, d), lambda i: (i, 0)), pl.BlockSpec((tm, p.shape[1]), lambda i: (i, 0)),
                  _resident(gains.shape), _resident(w_gate.shape), _resident(w_proj.shape)],
        out_specs=pl.BlockSpec((tm, d), lambda i: (i, 0)),
        out_shape=jax.ShapeDtypeStruct((n, d), F32),
        compiler_params=_params("parallel"),
        name="ple",
    )(x, p, gains, w_gate, w_proj)


def _even_in_kernel(x_ref, g_ref, win_ref, ra_ref, rb_ref, qn_ref, kvn_ref, wuq_ref, wbd_ref,
                    ret_ref, qc_ref, kc_ref, ckv_ref, kpe_ref):
    h = _rms(x_ref[...], g_ref[2:3]).astype(BF16)
    z = _dot(h, win_ref[...])
    hw = H_A * DK_A
    cosr, sinr = ra_ref[:, 0:LANES], ra_ref[:, LANES:2 * LANES]
    for hh in range(H_A):
        for base, sc in ((0, None), (hw, DK_A ** -0.5)):
            xh = z[:, base + hh * DK_A:base + (hh + 1) * DK_A]
            r = xh * cosr + pltpu.roll(xh, DK_A // 2, 1) * sinr
            ret_ref[:, base + hh * DK_A:base + (hh + 1) * DK_A] = r if sc is None else r * sc
    ret_ref[:, 2 * hw:4 * hw] = z[:, 2 * hw:4 * hw]

    c_cos, s_lo, s_hi = rb_ref[:, 0:LANES], rb_ref[:, LANES:2 * LANES], rb_ref[:, 2 * LANES:3 * LANES]

    def rope32(v):
        return (v * c_cos + pltpu.roll(v, LANES - ROPE_B // 2, 1) * s_lo
                + pltpu.roll(v, ROPE_B // 2, 1) * s_hi)

    o = 4 * hw
    cq = _rms(z[:, o:o + Q_RANK], qn_ref[...]).astype(BF16)
    q = _dot(cq, wuq_ref[...])
    nq = H_B * NOPE_B
    qlat = _dot(q[:, :nq].astype(BF16), wbd_ref[...])
    ckv = _rms(z[:, o + Q_RANK:o + Q_RANK + KV_RANK], kvn_ref[...])
    kpe = rope32(z[:, o + Q_RANK + KV_RANK:o + Q_RANK + KV_RANK + LANES])
    ckv_ref[...] = ckv
    kpe_ref[...] = kpe[:, :ROPE_B]
    kc_ref[:, 0:KV_RANK] = ckv.astype(BF16)
    kc_ref[:, KV_RANK:KV_RANK + LANES] = kpe.astype(BF16)
    for hh in range(H_B):
        qc_ref[hh, :, 0:KV_RANK] = qlat[:, hh * KV_RANK:(hh + 1) * KV_RANK].astype(BF16)
        qc_ref[hh, :, KV_RANK:KV_RANK + LANES] = rope32(q[:, nq + hh * LANES:nq + (hh + 1) * LANES]).astype(BF16)


def _even_in(x, gains, w_in, rope_a, rope_b, qn, kvn, w_uq, w_bd):
    n, d = x.shape
    tm = _row_tile(n)
    kw = KV_RANK + LANES
    row = lambda w: pl.BlockSpec((tm, w), lambda i: (i, 0))
    return pl.pallas_call(
        _even_in_kernel,
        grid=(n // tm,),
        in_specs=[row(d), _resident(gains.shape), _resident(w_in.shape), row(rope_a.shape[1]),
                  row(rope_b.shape[1]), _resident(qn.shape), _resident(kvn.shape), _resident(w_uq.shape),
                  _resident(w_bd.shape)],
        out_specs=[row(4 * H_A * DK_A), pl.BlockSpec((H_B, tm, kw), lambda i: (0, i, 0)), row(kw),
                   row(KV_RANK), row(ROPE_B)],
        out_shape=[jax.ShapeDtypeStruct((n, 4 * H_A * DK_A), F32), jax.ShapeDtypeStruct((H_B, n, kw), BF16),
                   jax.ShapeDtypeStruct((n, kw), BF16), jax.ShapeDtypeStruct((n, KV_RANK), F32),
                   jax.ShapeDtypeStruct((n, ROPE_B), F32)],
        compiler_params=_params("parallel"),
        name="even_in",
    )(x, gains, w_in, rope_a, rope_b, qn, kvn, w_uq, w_bd)


def _even_out_kernel(x_ref, or_ref, gr_ref, ol_ref, g_ref, gn_ref, wuv_ref, wout_ref, o_ref):
    hw = H_A * DV_A
    gate = gr_ref[...]
    gate = gate * jax.nn.sigmoid(gate)
    y = None
    for hh in range(H_A):
        sl = slice(hh * DV_A, (hh + 1) * DV_A)
        oh = or_ref[:, sl]
        xc = oh - jnp.mean(oh, axis=-1, keepdims=True)
        var = jnp.mean(xc * xc, axis=-1, keepdims=True)
        a = gate[:, sl] * (xc * lax.rsqrt(var + NORM_EPS) * gn_ref[:, sl])
        t = _dot(a.astype(BF16), wout_ref[sl, :])
        y = t if y is None else y + t
    om = None
    for hh in range(H_B):
        t = _dot(ol_ref[hh], wuv_ref[hh])
        om = t if om is None else om + t
    y = y + _dot(om.astype(BF16), wout_ref[hw:, :])
    o_ref[...] = x_ref[...] + _rms(y, g_ref[3:4])


def _even_out(x, o_r, ret, o_lat, gains, gn, w_uv, w_out):
    n, d = x.shape
    tm = _row_tile(n)
    hw = H_A * DV_A
    return pl.pallas_call(
        _even_out_kernel,
        grid=(n // tm,),
        in_specs=[pl.BlockSpec((tm, d), lambda i: (i, 0)), pl.BlockSpec((tm, hw), lambda i: (i, 0)),
                  pl.BlockSpec((tm, hw), lambda i: (i, 3)),
                  pl.BlockSpec((H_B, tm, KV_RANK), lambda i: (0, i, 0)),
                  _resident(gains.shape), _resident(gn.shape), _resident(w_uv.shape), _resident(w_out.shape)],
        out_specs=pl.BlockSpec((tm, d), lambda i: (i, 0)),
        out_shape=jax.ShapeDtypeStruct((n, d), F32),
        compiler_params=_params("parallel"),
        name="even_out",
    )(x, o_r, ret, o_lat, gains, gn, w_uv, w_out)


def _odd_in_kernel(x_ref, g_ref, win_ref, bf_ref, qkv_ref, fk_ref, fv_ref, fl_ref):
    h = _rms(x_ref[...], g_ref[2:3]).astype(BF16)
    z = _dot(h, win_ref[...])
    w = H_C * HD_C
    qkv_ref[:, 0:w] = (z[:, 0:w] * FOX_SCALE).astype(BF16)
    qkv_ref[:, w:3 * w] = z[:, w:3 * w].astype(BF16)
    fk_ref[...] = z[:, w:2 * w]
    fv_ref[...] = z[:, 2 * w:3 * w]
    t = z[:, 3 * w:3 * w + LANES] + bf_ref[...]
    ls = jnp.minimum(t, 0.0) - jnp.log1p(jnp.exp(-jnp.abs(t)))
    fl_ref[...] = ls[:, :H_C]


def _odd_in(x, gains, w_in, b_f):
    n, d = x.shape
    tm = _row_tile(n)
    w = H_C * HD_C
    row = lambda c: pl.BlockSpec((tm, c), lambda i: (i, 0))
    return pl.pallas_call(
        _odd_in_kernel,
        grid=(n // tm,),
        in_specs=[row(d), _resident(gains.shape), _resident(w_in.shape), _resident(b_f.shape)],
        out_specs=[row(3 * w), row(w), row(w), row(H_C)],
        out_shape=[jax.ShapeDtypeStruct((n, 3 * w), BF16), jax.ShapeDtypeStruct((n, w), F32),
                   jax.ShapeDtypeStruct((n, w), F32), jax.ShapeDtypeStruct((n, H_C), F32)],
        compiler_params=_params("parallel"),
        name="odd_in",
    )(x, gains, w_in, b_f)


def _odd_out_kernel(x_ref, a_ref, g_ref, wout_ref, o_ref):
    o_ref[...] = x_ref[...] + _rms(_dot(a_ref[...], wout_ref[...]), g_ref[3:4])


def _odd_out(x, a, gains, w_out):
    n, d = x.shape
    tm = _row_tile(n)
    return pl.pallas_call(
        _odd_out_kernel,
        grid=(n // tm,),
        in_specs=[pl.BlockSpec((tm, d), lambda i: (i, 0)), pl.BlockSpec((tm, a.shape[1]), lambda i: (i, 0)),
                  _resident(gains.shape), _resident(w_out.shape)],
        out_specs=pl.BlockSpec((tm, d), lambda i: (i, 0)),
        out_shape=jax.ShapeDtypeStruct((n, d), F32),
        compiler_params=_params("parallel"),
        name="odd_out",
    )(x, a, gains, w_out)


def _ret_kernel(qkv_ref, s0_ref, dec_ref, qd_ref, kd_ref, sd_ref, o_ref, s_ref):
    @pl.when(pl.program_id(1) == 0)
    def _():
        s_ref[...] = s0_ref[...]

    for hh in range(H_A):
        q = qkv_ref[:, hh * DK_A:(hh + 1) * DK_A]
        k = qkv_ref[:, (H_A + hh) * DK_A:(H_A + hh + 1) * DK_A]
        v = qkv_ref[:, 2 * H_A * DK_A + hh * DV_A:2 * H_A * DK_A + (hh + 1) * DV_A].astype(BF16)
        s = s_ref[0, hh]
        qb = q.astype(BF16)
        scores = _dot_nt(qb, k.astype(BF16)) * dec_ref[hh]
        o = _dot(scores.astype(BF16), v) + _dot(qb, s.astype(BF16)) * qd_ref[hh]
        o_ref[:, hh * DV_A:(hh + 1) * DV_A] = o
        s_ref[0, hh] = sd_ref[hh] * s + _dot_tn((k * kd_ref[hh]).astype(BF16), v)


def _retention(qkv, s0, nb, nc, length):
    lp = RET_CHUNK
    log_gamma = jnp.log1p(-jnp.exp2(-5.0 - jnp.arange(H_A, dtype=F32)))
    idx = jnp.arange(lp, dtype=F32)
    diff = idx[:, None] - idx[None, :]
    decay = jnp.where(diff >= 0, jnp.exp(log_gamma[:, None, None] * jnp.maximum(diff, 0.0)), 0.0)
    ones = jnp.ones((1, 1, DV_A), F32)
    qd = jnp.exp(log_gamma[:, None] * (idx[None, :] + 1.0))[:, :, None] * ones
    kd = jnp.exp(log_gamma[:, None] * (length - 1.0 - idx[None, :]))[:, :, None] * ones
    sd = jnp.exp(log_gamma * length)[:, None, None] * ones
    wq = 3 * H_A * DK_A
    return pl.pallas_call(
        _ret_kernel,
        grid=(nb, nc),
        in_specs=[pl.BlockSpec((lp, wq), lambda b, c: (b * nc + c, 0)),
                  pl.BlockSpec((1, H_A, DK_A, DV_A), lambda b, c: (b, 0, 0, 0)),
                  _resident(decay.shape), _resident(qd.shape), _resident(kd.shape), _resident(sd.shape)],
        out_specs=[pl.BlockSpec((lp, H_A * DV_A), lambda b, c: (b * nc + c, 0)),
                   pl.BlockSpec((1, H_A, DK_A, DV_A), lambda b, c: (b, 0, 0, 0))],
        out_shape=[jax.ShapeDtypeStruct((nb * nc * lp, H_A * DV_A), F32),
                   jax.ShapeDtypeStruct((nb, H_A, DK_A, DV_A), F32)],
        compiler_params=_params("parallel", "arbitrary"),
        name="retention",
    )(qkv, s0, decay, qd, kd, sd)


def _online_update(m_ref, l_ref, acc_ref, s, pv):
    m_prev = m_ref[...]
    m_new = jnp.maximum(m_prev, jnp.max(s, axis=1, keepdims=True))
    alpha = jnp.exp(m_prev - m_new)
    p = jnp.exp(s - m_new)
    l_ref[...] = alpha * l_ref[...] + jnp.sum(p, axis=1, keepdims=True)
    acc_ref[...] = alpha * acc_ref[...] + pv(p.astype(BF16))
    m_ref[...] = m_new


def _init_state(m_ref, l_ref, acc_ref):
    m_ref[...] = jnp.full(m_ref.shape, NEG_INF, F32)
    l_ref[...] = jnp.zeros(l_ref.shape, F32)
    acc_ref[...] = jnp.zeros(acc_ref.shape, F32)


def _mla_flash_kernel(q_ref, k_ref, o_ref, m_ref, acc_ref, s_ref, *, tq, tk):
    i = pl.program_id(1)
    rows = H_B * tq
    q = q_ref[...].reshape(rows, q_ref.shape[2])
    m_ref[...] = jnp.full(m_ref.shape, NEG_INF, F32)
    acc_ref[...] = jnp.zeros(acc_ref.shape, F32)
    ones = jnp.ones((tk, LANES), BF16)
    c = MLA_SCALE * LOG2E

    def scores(slot, j):
        s_ref[slot] = _dot_nt(q, k_ref[pl.ds(pl.multiple_of(j * tk, tk), tk), :])

    def update(slot, j, masked):
        off = pl.multiple_of(j * tk, tk)
        s = s_ref[slot]
        if masked:
            qpos = i * tq + (lax.broadcasted_iota(jnp.int32, (rows, tk), 0) & (tq - 1))
            kpos = off + lax.broadcasted_iota(jnp.int32, (rows, tk), 1)
            s = jnp.where(kpos <= qpos, s, NEG_INF)
        m_prev = m_ref[...]
        m_new = jnp.maximum(m_prev, jnp.max(s, axis=1, keepdims=True))
        p = jnp.exp2(((s - m_new) * c).astype(BF16))
        v = jnp.concatenate([k_ref[pl.ds(off, tk), 0:KV_RANK], ones], axis=1)
        acc_ref[...] = jnp.exp2((m_prev - m_new) * c) * acc_ref[...] + _dot(p, v)
        m_ref[...] = m_new

    nfull = (i * tq) // tk
    npair = nfull // 2
    scores(0, 0)

    def body(jj, carry):
        j = 2 * jj
        scores(1, j + 1)
        update(0, j, False)
        scores(0, j + 2)
        update(1, j + 1, False)
        return carry

    lax.fori_loop(0, npair, body, 0)

    @pl.when(nfull % 2 == 1)
    def _():
        scores(1, nfull)
        update(0, nfull - 1, False)
        update(1, nfull, True)

    @pl.when(nfull % 2 == 0)
    def _():
        update(0, nfull, True)

    o = acc_ref[:, 0:KV_RANK] / acc_ref[:, KV_RANK:KV_RANK + LANES]
    o_ref[...] = o.reshape(H_B, tq, KV_RANK).astype(BF16)


def _mla_prompt(qc, kc, nb, seq):
    tq = min(256, seq)
    tk = min(512, seq)
    nq = seq // tq
    kw = kc.shape[1]
    return pl.pallas_call(
        functools.partial(_mla_flash_kernel, tq=tq, tk=tk),
        grid=(nb, nq),
        in_specs=[pl.BlockSpec((H_B, tq, kw), lambda b, i: (0, b * nq + i, 0)),
                  pl.BlockSpec((seq, kw), lambda b, i: (b, 0))],
        out_specs=pl.BlockSpec((H_B, tq, KV_RANK), lambda b, i: (0, b * nq + i, 0)),
        out_shape=jax.ShapeDtypeStruct((H_B, nb * seq, KV_RANK), BF16),
        scratch_shapes=[pltpu.VMEM((H_B * tq, 1), F32), pltpu.VMEM((H_B * tq, KV_RANK + LANES), F32),
                        pltpu.VMEM((2, H_B * tq, tk), F32)],
        compiler_params=_params("parallel", "arbitrary"),
        name="mla_prompt",
    )(qc, kc)


def _cumsum_lanes(x, tri):
    hi = x.astype(BF16)
    r = x - hi.astype(F32)
    mid = r.astype(BF16)
    lo = (r - mid.astype(F32)).astype(BF16)
    cc = _dot(jnp.concatenate([hi, mid, lo], axis=0), tri)
    n = x.shape[0]
    return cc[0:n] + cc[n:2 * n] + cc[2 * n:3 * n]


def _cumsum_kernel(x_ref, tri_ref, o_ref, c_ref):
    @pl.when(pl.program_id(1) == 0)
    def _():
        c_ref[...] = jnp.zeros(c_ref.shape, F32)

    c = _cumsum_lanes(x_ref[0], tri_ref[...]) + c_ref[...]
    o_ref[0] = c
    c_ref[...] = jnp.broadcast_to(c[:, LANES - 1:LANES], c_ref.shape)


def _cumsum_seq(x_t, tri):
    nb, h, seq = x_t.shape
    return pl.pallas_call(
        _cumsum_kernel,
        grid=(nb, seq // LANES),
        in_specs=[pl.BlockSpec((1, h, LANES), lambda b, j: (b, 0, j)), _resident(tri.shape)],
        out_specs=pl.BlockSpec((1, h, LANES), lambda b, j: (b, 0, j)),
        out_shape=jax.ShapeDtypeStruct((nb, h, seq), F32),
        scratch_shapes=[pltpu.VMEM((h, LANES), F32)],
        compiler_params=_params("parallel", "arbitrary"),
        name="cumsum",
    )(x_t, tri)


def _fox_flash_kernel(q_ref, k_ref, v_ref, c_ref, o_ref, ma, acca, mb, accb, s_ref, *, tq, tk):
    i = pl.program_id(2)
    q2 = q_ref[...]
    lane = lax.broadcasted_iota(jnp.int32, q2.shape, 1)
    zero = jnp.zeros_like(q2)
    qa = jnp.where(lane < HD_C, q2, zero)
    qb = jnp.where(lane >= HD_C, q2, zero)
    for m_ref, acc_ref in ((ma, acca), (mb, accb)):
        m_ref[...] = jnp.full(m_ref.shape, NEG_INF, F32)
        acc_ref[...] = jnp.zeros(acc_ref.shape, F32)
    lane_v = lax.broadcasted_iota(jnp.int32, (tk, LANES), 1)
    one = jnp.ones((tk, LANES), BF16)

    def scores(slot, j):
        off = pl.multiple_of(j * tk, tk)
        k = k_ref[pl.ds(off, tk), :]
        cb = c_ref[0, :, pl.ds(off, tk)]
        s_ref[slot, 0] = _dot_nt(qa, k) - cb[0:1, :]
        s_ref[slot, 1] = _dot_nt(qb, k) - cb[1:2, :]

    def update(slot, j, masked):
        off = pl.multiple_of(j * tk, tk)
        v = v_ref[pl.ds(off, tk), :]
        vs = (jnp.where(lane_v < HD_C, v, one), jnp.where(lane_v >= HD_C, v, one))
        if masked:
            qpos = i * tq + lax.broadcasted_iota(jnp.int32, (tq, tk), 0)
            kpos = off + lax.broadcasted_iota(jnp.int32, (tq, tk), 1)
            keep = kpos <= qpos
        for h, (vx, m_ref, acc_ref) in enumerate(zip(vs, (ma, mb), (acca, accb))):
            s = s_ref[slot, h]
            if masked:
                s = jnp.where(keep, s, NEG_INF)
            m_prev = m_ref[...]
            m_new = jnp.maximum(m_prev, jnp.max(s, axis=1, keepdims=True))
            p = jnp.exp((s - m_new).astype(BF16))
            acc_ref[...] = jnp.exp(m_prev - m_new) * acc_ref[...] + _dot(p, vx)
            m_ref[...] = m_new

    nfull = (i * tq) // tk
    npair = nfull // 2
    scores(0, 0)

    def body(jj, carry):
        j = 2 * jj
        scores(1, j + 1)
        update(0, j, False)
        scores(0, j + 2)
        update(1, j + 1, False)
        return carry

    lax.fori_loop(0, npair, body, 0)

    @pl.when(nfull % 2 == 1)
    def _():
        scores(1, nfull)
        update(0, nfull - 1, False)
        update(1, nfull, True)

    @pl.when(nfull % 2 == 0)
    def _():
        update(0, nfull, True)

    a, b = acca[...], accb[...]
    lane_o = lax.broadcasted_iota(jnp.int32, (tq, LANES), 1)
    o = jnp.where(lane_o < HD_C, a / pltpu.roll(a, HD_C, 1), b / pltpu.roll(b, HD_C, 1))
    o_ref[...] = o.astype(BF16)


def _fox_prompt(qkv, c_pairs, nb, seq):
    tq = min(512, seq)
    tk = min(512, seq)
    assert tk % tq == 0
    nq = seq // tq
    npair = H_C * HD_C // LANES
    st = lambda w: pltpu.VMEM((tq, w), F32)
    return pl.pallas_call(
        functools.partial(_fox_flash_kernel, tq=tq, tk=tk),
        grid=(nb, npair, nq),
        in_specs=[pl.BlockSpec((tq, LANES), lambda b, h, i: (b * nq + i, h)),
                  pl.BlockSpec((seq, LANES), lambda b, h, i: (b, npair + h)),
                  pl.BlockSpec((seq, LANES), lambda b, h, i: (b, 2 * npair + h)),
                  pl.BlockSpec((1, 2, seq), lambda b, h, i: (b * npair + h, 0, 0))],
        out_specs=pl.BlockSpec((tq, LANES), lambda b, h, i: (b * nq + i, h)),
        out_shape=jax.ShapeDtypeStruct((nb * seq, H_C * HD_C), BF16),
        scratch_shapes=[st(1), st(LANES), st(1), st(LANES), pltpu.VMEM((2, 2, tq, tk), F32)],
        compiler_params=_params("parallel", "parallel", "arbitrary"),
        name="fox_prompt",
    )(qkv, qkv, qkv, c_pairs)


def _mla_dec_kernel(pt_ref, q_ref, *refs, pp, t_new):
    del pt_ref
    ck_refs, kp_refs = refs[:pp], refs[pp:2 * pp]
    kn_ref, o_ref, m_ref, l_ref, acc_ref = refs[2 * pp:]
    c = pl.program_id(1)

    @pl.when(c == 0)
    def _():
        _init_state(m_ref, l_ref, acc_ref)

    q = q_ref[0]
    ql, qp = q[:, :KV_RANK], q[:, KV_RANK:KV_RANK + ROPE_B]
    cks = [r[0, 0].astype(BF16) for r in ck_refs]
    s = jnp.concatenate([_dot_nt(ql, ck) + _dot(qp, kp[0, 0].astype(BF16))
                         for ck, kp in zip(cks, kp_refs)], axis=1) * MLA_SCALE

    def pv(p):
        out = None
        for n, ck in enumerate(cks):
            t = _dot(p[:, n * LANES:(n + 1) * LANES], ck)
            out = t if out is None else out + t
        return out

    _online_update(m_ref, l_ref, acc_ref, s, pv)

    @pl.when(c == pl.num_programs(1) - 1)
    def _():
        kn = kn_ref[0]
        sn = _dot_nt(q, kn) * MLA_SCALE
        tok = lax.broadcasted_iota(jnp.int32, sn.shape, 0) & (t_new - 1)
        col = lax.broadcasted_iota(jnp.int32, sn.shape, 1)
        sn = jnp.where(col <= tok, sn, NEG_INF)
        _online_update(m_ref, l_ref, acc_ref, sn, lambda p: _dot(p, kn[:, :KV_RANK]))
        o_ref[0] = acc_ref[...] / l_ref[...]


def _mla_decode(page_table, q, pool_ckv, pool_kpe, k_new, layer, t_new):
    bs, rows, kw = q.shape
    n_pages = page_table.shape[1]
    page = pool_ckv.shape[2]
    pp = min(16, n_pages)
    assert page == LANES and n_pages % pp == 0 and t_new & (t_new - 1) == 0
    nch = n_pages // pp
    pt = page_table.reshape(-1)

    def pool_spec(shape, n):
        return pl.BlockSpec((1, 1) + shape,
                            lambda b, c, pt_ref: (layer, pt_ref[b * n_pages + c * pp + n], 0, 0))

    in_specs = ([pl.BlockSpec((1, rows, kw), lambda b, c, pt_ref: (b, 0, 0))]
                + [pool_spec((page, KV_RANK), n) for n in range(pp)]
                + [pool_spec((ROPE_B, page), n) for n in range(pp)]
                + [pl.BlockSpec((1, page, kw), lambda b, c, pt_ref: (b, 0, 0))])
    return pl.pallas_call(
        functools.partial(_mla_dec_kernel, pp=pp, t_new=t_new),
        grid_spec=pltpu.PrefetchScalarGridSpec(
            num_scalar_prefetch=1, grid=(bs, nch), in_specs=in_specs,
            out_specs=pl.BlockSpec((1, rows, KV_RANK), lambda b, c, pt_ref: (b, 0, 0)),
            scratch_shapes=[pltpu.VMEM((rows, 1), F32), pltpu.VMEM((rows, 1), F32),
                            pltpu.VMEM((rows, KV_RANK), F32)]),
        out_shape=jax.ShapeDtypeStruct((bs, rows, KV_RANK), F32),
        compiler_params=_params("parallel", "arbitrary"),
        name="mla_decode",
    )(pt, q, *([pool_ckv] * pp), *([pool_kpe] * pp), k_new)


def _fox_dec_kernel(pt_ref, q_ref, *refs, pp, t_new):
    del pt_ref
    k_refs, v_refs, lf_refs = refs[:pp], refs[pp:2 * pp], refs[2 * pp:3 * pp]
    kn_ref, vn_ref, lfn_ref, tri_ref, o_ref, m_ref, l_ref, acc_ref, c_ref = refs[3 * pp:]
    c = pl.program_id(1)

    @pl.when(c == 0)
    def _():
        _init_state(m_ref, l_ref, acc_ref)
        c_ref[...] = jnp.zeros(c_ref.shape, F32)

    q = q_ref[0]
    tri = tri_ref[...]

    def scores(k_bf16, cpage):
        return _dot(q, k_bf16) - jnp.concatenate([cpage] * t_new, axis=0)

    def prefixes(lf_pages):
        local = _cumsum_lanes(jnp.concatenate(lf_pages, axis=0), tri)
        carry = c_ref[...]
        out = []
        for n in range(len(lf_pages)):
            cpage = local[n * H_C:(n + 1) * H_C] + carry
            carry = jnp.broadcast_to(cpage[:, LANES - 1:LANES], carry.shape)
            out.append(cpage)
        c_ref[...] = carry
        return out

    def pv_pages(vs):
        def pv(p):
            out = None
            for n, v in enumerate(vs):
                t = _dot_nt(p[:, n * LANES:(n + 1) * LANES], v)
                out = t if out is None else out + t
            return out
        return pv

    cps = prefixes([r[0, 0] for r in lf_refs])
    s = jnp.concatenate([scores(kr[0, 0].astype(BF16), cp) for kr, cp in zip(k_refs, cps)], axis=1)
    _online_update(m_ref, l_ref, acc_ref, s, pv_pages([r[0, 0].astype(BF16) for r in v_refs]))

    @pl.when(c == pl.num_programs(1) - 1)
    def _():
        sn = scores(kn_ref[0], prefixes([lfn_ref[0]])[0])
        tok = lax.broadcasted_iota(jnp.int32, sn.shape, 0) // H_C
        col = lax.broadcasted_iota(jnp.int32, sn.shape, 1)
        sn = jnp.where(col <= tok, sn, NEG_INF)
        _online_update(m_ref, l_ref, acc_ref, sn, pv_pages([vn_ref[0]]))
        o = acc_ref[...] / l_ref[...]
        row_h = lax.broadcasted_iota(jnp.int32, o.shape, 0) % H_C
        lane_h = lax.broadcasted_iota(jnp.int32, o.shape, 1) // HD_C
        o = jnp.where(row_h == lane_h, o, 0.0)
        o_ref[0] = jnp.sum(o.reshape(t_new, H_C, o.shape[1]), axis=1)


def _fox_decode(page_table, q_bd, pool_k_t, pool_v_t, pool_lf_t, k_new_t, v_new_t, lf_new_t, tri, layer, t_new):
    bs, rows, w = q_bd.shape
    n_pages = page_table.shape[1]
    page = pool_k_t.shape[3]
    pp = min(16, n_pages)
    assert page == LANES and n_pages % pp == 0
    nch = n_pages // pp
    pt = page_table.reshape(-1)

    def pool_spec(shape, n):
        return pl.BlockSpec((1, 1) + shape,
                            lambda b, c, pt_ref: (layer, pt_ref[b * n_pages + c * pp + n], 0, 0))

    per_seq = lambda shape: pl.BlockSpec((1,) + shape, lambda b, c, pt_ref: (b, 0, 0))
    in_specs = ([per_seq((rows, w))]
                + [pool_spec((w, page), n) for n in range(pp)] + [pool_spec((w, page), n) for n in range(pp)]
                + [pool_spec((H_C, page), n) for n in range(pp)]
                + [per_seq((w, page)), per_seq((w, page)), per_seq((H_C, page)),
                   pl.BlockSpec(tri.shape, lambda b, c, pt_ref: (0, 0))])
    return pl.pallas_call(
        functools.partial(_fox_dec_kernel, pp=pp, t_new=t_new),
        grid_spec=pltpu.PrefetchScalarGridSpec(
            num_scalar_prefetch=1, grid=(bs, nch), in_specs=in_specs,
            out_specs=per_seq((t_new, w)),
            scratch_shapes=[pltpu.VMEM((rows, 1), F32), pltpu.VMEM((rows, 1), F32),
                            pltpu.VMEM((rows, w), F32), pltpu.VMEM((H_C, LANES), F32)]),
        out_shape=jax.ShapeDtypeStruct((bs, t_new, w), F32),
        compiler_params=_params("parallel", "arbitrary"),
        name="fox_decode",
    )(pt, q_bd, *([pool_k_t] * pp), *([pool_v_t] * pp), *([pool_lf_t] * pp), k_new_t, v_new_t, lf_new_t, tri)


def _rope_tables(pos):
    pos = pos.astype(F32)[:, None]

    def cs(half):
        inv = ROPE_THETA ** (-jnp.arange(half, dtype=F32) / half)
        ang = pos * inv[None, :]
        return jnp.cos(ang), jnp.sin(ang)

    c, s = cs(DK_A // 2)
    rope_a = jnp.concatenate([c, c, -s, s], axis=1)
    c, s = cs(ROPE_B // 2)
    z = lambda w: jnp.zeros((pos.shape[0], w), F32)
    rope_b = jnp.concatenate([c, c, z(LANES - ROPE_B), -s, z(LANES - ROPE_B // 2),
                              z(ROPE_B // 2), s, z(LANES - ROPE_B)], axis=1)
    return rope_a, rope_b


def _pad_cols(w, width):
    return jnp.pad(w, ((0, 0), (0, width - w.shape[1])))


def kernel(x_prompt, x_sample, state_ret, cache_mla_ckv, cache_mla_kpe, cache_fox_k, cache_fox_v,
           cache_fox_logf, page_table, p_prompt, p_sample, norm_gains, ffn_w_in, ffn_w_out, ple_w_gate,
           ple_w_proj, w_in0, ret_gn, mla_q_norm, mla_w_uq, mla_kv_norm, mla_w_uk, mla_w_uv, w_out0,
           w_in1, fox_b_f, w_out1):
    nb, seq, d = x_prompt.shape
    bs, t_new, _ = x_sample.shape
    depth = norm_gains.shape[0]
    n_p, n_s = nb * seq, bs * t_new
    n_pages, page = page_table.shape[1], cache_mla_ckv.shape[2]
    past_len = n_pages * page
    assert seq % RET_CHUNK == 0 and t_new <= RET_CHUNK

    x = jnp.concatenate([x_prompt.reshape(n_p, d), x_sample.reshape(n_s, d)], axis=0)
    pos = jnp.concatenate([jnp.tile(jnp.arange(seq, dtype=jnp.int32), nb),
                           jnp.tile(past_len + jnp.arange(t_new, dtype=jnp.int32), bs)])
    rope_a, rope_b = _rope_tables(pos)
    eye_b = jnp.eye(H_B, dtype=F32)
    ii = jnp.arange(LANES)
    tri = (ii[:, None] <= ii[None, :]).astype(BF16)

    rets_p, rets_s, ckvs, kpes, fks, fvs, fls = [], [], [], [], [], [], []
    for i in range(depth):
        g = norm_gains[i]
        j = i // 2
        x = _ffn(x, g, ffn_w_in[i, 0].astype(BF16), ffn_w_out[i, 0].astype(BF16), 0)
        if i % 2 == 0:
            w_in = _pad_cols(w_in0[j], IN0_PAD).astype(BF16)
            uq = mla_w_uq[j]
            w_uq = jnp.concatenate(
                [uq[:, :, :NOPE_B].reshape(Q_RANK, H_B * NOPE_B),
                 jnp.pad(uq[:, :, NOPE_B:], ((0, 0), (0, 0), (0, LANES - ROPE_B))).reshape(Q_RANK, H_B * LANES)],
                axis=1).astype(BF16)
            w_bd = jnp.einsum('hdr,hg->hdgr', jnp.transpose(mla_w_uk[j], (1, 2, 0)), eye_b)
            w_bd = w_bd.reshape(H_B * NOPE_B, H_B * KV_RANK).astype(BF16)
            w_uv = jnp.einsum('rhe,hg->hrge', mla_w_uv[j], eye_b).reshape(H_B, KV_RANK, H_B * VD_B).astype(BF16)
            ret, qc, kc, ckv, kpe = _even_in(x, g, w_in, rope_a, rope_b, mla_q_norm[j][None], mla_kv_norm[j][None],
                                             w_uq, w_bd)
            wq = 3 * H_A * DK_A
            o_rp, s_p = _retention(ret, jnp.zeros((nb, H_A, DK_A, DV_A), F32), nb, seq // RET_CHUNK, float(RET_CHUNK))
            ret_s = jnp.pad(ret[n_p:, :wq].reshape(bs, t_new, wq), ((0, 0), (0, RET_CHUNK - t_new), (0, 0)))
            o_rs, s_s = _retention(ret_s.reshape(bs * RET_CHUNK, wq), state_ret[j], bs, 1, float(t_new))
            o_rs = o_rs.reshape(bs, RET_CHUNK, H_A * DV_A)[:, :t_new].reshape(n_s, H_A * DV_A)
            o_r = jnp.concatenate([o_rp, o_rs], axis=0)
            ol_p = _mla_prompt(qc, kc, nb, seq)
            kw = kc.shape[1]
            q_s = qc[:, n_p:].reshape(H_B, bs, t_new, kw).transpose(1, 0, 2, 3).reshape(bs, H_B * t_new, kw)
            k_new = jnp.pad(kc[n_p:].reshape(bs, t_new, kw), ((0, 0), (0, page - t_new), (0, 0)))
            ol_s = _mla_decode(page_table, q_s, cache_mla_ckv, cache_mla_kpe.transpose(0, 1, 3, 2), k_new, j, t_new)
            ol_s = ol_s.reshape(bs, H_B, t_new, KV_RANK).transpose(1, 0, 2, 3).reshape(H_B, n_s, KV_RANK)
            o_lat = jnp.concatenate([ol_p, ol_s.astype(BF16)], axis=1)
            x = _even_out(x, o_r, ret, o_lat, g, ret_gn[j].reshape(1, H_A * DV_A), w_uv, w_out0[j].astype(BF16))
            rets_p.append(s_p)
            rets_s.append(s_s)
            ckvs.append(ckv)
            kpes.append(kpe)
        else:
            w = H_C * HD_C
            w_in = _pad_cols(w_in1[j], IN1_PAD).astype(BF16)
            b_f = jnp.pad(fox_b_f[j], (0, LANES - H_C))[None]
            qkv, fk, fv, fl = _odd_in(x, g, w_in, b_f)
            c_t = _cumsum_seq(fl[:n_p].reshape(nb, seq, H_C).transpose(0, 2, 1), tri)
            o_p = _fox_prompt(qkv, c_t.reshape(nb * H_C // 2, 2, seq), nb, seq)
            q4 = qkv[n_p:, :w].reshape(bs, t_new, H_C, 1, HD_C)
            q_bd = (q4 * jnp.eye(H_C, dtype=BF16)[None, None, :, :, None]).reshape(bs, t_new * H_C, w)
            keys_last = lambda a, c: jnp.pad(a.reshape(bs, t_new, c).transpose(0, 2, 1),
                                             ((0, 0), (0, 0), (0, page - t_new)))
            n_pool = cache_fox_k.shape[1]
            pool_t = lambda a: a.transpose(0, 1, 3, 4, 2).reshape(-1, n_pool, w, page)
            o_s = _fox_decode(page_table, q_bd, pool_t(cache_fox_k), pool_t(cache_fox_v),
                              cache_fox_logf.transpose(0, 1, 3, 2), keys_last(qkv[n_p:, w:2 * w], w),
                              keys_last(qkv[n_p:, 2 * w:], w), keys_last(fl[n_p:], H_C), tri, j, t_new)
            a = jnp.concatenate([o_p, o_s.reshape(n_s, w).astype(BF16)], axis=0)
            x = _odd_out(x, a, g, w_out1[j].astype(BF16))
            fks.append(fk)
            fvs.append(fv)
            fls.append(fl)
        x = _ffn(x, g, ffn_w_in[i, 1].astype(BF16), ffn_w_out[i, 1].astype(BF16), 4)
        p = jnp.concatenate([p_prompt[i].reshape(n_p, -1), p_sample[i].reshape(n_s, -1)], axis=0)
        x = _ple(x, p, g, ple_w_gate[i].astype(BF16), ple_w_proj[i].astype(BF16))

    def split(parts, tail):
        a = jnp.stack(parts)
        return (a[:, :n_p].reshape((len(parts), nb, seq) + tail),
                a[:, n_p:].reshape((len(parts), bs, t_new) + tail))

    ckv_p, ckv_s = split(ckvs, (KV_RANK,))
    kpe_p, kpe_s = split(kpes, (ROPE_B,))
    fk_p, fk_s = split(fks, (H_C, HD_C))
    fv_p, fv_s = split(fvs, (H_C, HD_C))
    fl_p, fl_s = split(fls, (H_C,))
    return (x[:n_p].reshape(nb, seq, d), x[n_p:].reshape(bs, t_new, d), jnp.stack(rets_p), jnp.stack(rets_s),
            ckv_p, ckv_s, kpe_p, kpe_s, fk_p, fk_s, fv_p, fv_s, fl_p, fl_s)
```
